```python
import jax, jax.numpy as jnp
from jax import lax
import numpy as np

D_MODEL = 1024
BATCH = 2
SEQ = 16384
DEPTH = 4

GRID_W = 64
CTX_LEN = 256
HEAD_DIM = 64
ROPE_THETA = 10000.0
Q_BLOCK = 128
RMS_EPS = 1e-6
H_A = 8
NOPE_A = 64
ROPE_A = 32
V_A = 64
Q_LORA = 384
KV_LORA = 256
H_B = 8
KV_B = 2
G_B = H_B // KV_B
AB_IN = Q_LORA + KV_LORA + ROPE_A + H_B * HEAD_DIM + 2 * KV_B * HEAD_DIM
AB_OUT = H_A * V_A + H_B * HEAD_DIM
H_C = 16
WIN_R = 8
WIN_C = 16
C_WIDTH = H_C * HEAD_DIM
N_GROUPS = 4
EXPERTS_PER_GROUP = 8
N_EXPERTS = N_GROUPS * EXPERTS_PER_GROUP
TOP_K = 2
D_EXPERT = 512
MOE_BLOCK = 128
N_EVEN = (DEPTH + 1) // 2
N_ODD = DEPTH // 2

kernel_name = "hybrid_mla_gqa_natten_hmoe_prefix_dit"


def rmsnorm(x, g):
    xf = x.astype(jnp.float32)
    y = xf * lax.rsqrt(jnp.mean(xf * xf, axis=-1, keepdims=True) + RMS_EPS)
    return (y * g.astype(jnp.float32)).astype(x.dtype)


def modulate(x, g, shift, scale):
    return rmsnorm(x, g) * (1 + scale) + shift


def axial_rope_table(rows, cols, rot_dim):
    n = rot_dim // 4
    inv = ROPE_THETA ** (-jnp.arange(n, dtype=jnp.float32) / n)
    ang = jnp.concatenate([rows[:, None] * inv, cols[:, None] * inv], axis=-1)
    return jnp.cos(ang), jnp.sin(ang)


def apply_rope(x, rope):
    cos, sin = rope
    cos = cos.astype(x.dtype)
    sin = sin.astype(x.dtype)
    x1 = x[..., 0::2]
    x2 = x[..., 1::2]
    return jnp.stack([x1 * cos - x2 * sin, x1 * sin + x2 * cos], axis=-1).reshape(x.shape)


def heads_to_tokens(o):
    b, h, t, d = o.shape
    return o.transpose(0, 2, 1, 3).reshape(b, t, h * d)


def attend(q, k, v):
    s = jnp.einsum('bhgqd,bhkd->bhgqk', q, k, preferred_element_type=jnp.float32)
    p = jax.nn.softmax(s, axis=-1).astype(v.dtype)
    return jnp.einsum('bhgqk,bhkd->bhgqd', p, v)


def prefix_attention(q, k_lat, v_lat, k_ctx, v_ctx):
    b, hk, g, s, d = q.shape
    nb = s // Q_BLOCK
    k_all = jnp.concatenate([k_ctx, k_lat], axis=2)
    v_all = jnp.concatenate([v_ctx, v_lat], axis=2)
    qb = q.reshape(b, hk, g, nb, Q_BLOCK, d).transpose(3, 0, 1, 2, 4, 5)
    o = lax.map(lambda qblk: attend(qblk, k_all, v_all), qb)
    return o.transpose(1, 2, 3, 0, 4, 5).reshape(b, hk, g, s, v_all.shape[-1])


def mixer_ab(h_ctx, h_lat, rope_a, rope_b, w_in, w_out, q_norm, w_q_up, kv_norm, w_kv_up, gq_norm, gk_norm, need_ctx):
    def project(h, ra, rb):
        bsz, t, _ = h.shape
        p = h @ w_in
        o0 = Q_LORA
        o1 = o0 + KV_LORA
        o2 = o1 + ROPE_A
        o3 = o2 + H_B * HEAD_DIM
        o4 = o3 + KV_B * HEAD_DIM
        cq, ckv, kr = p[..., :o0], p[..., o0:o1], p[..., o1:o2]
        gq, gk, gv = p[..., o2:o3], p[..., o3:o4], p[..., o4:]
        qa = (rmsnorm(cq, q_norm) @ w_q_up).reshape(bsz, t, H_A, NOPE_A + ROPE_A).transpose(0, 2, 1, 3)
        kva = (rmsnorm(ckv, kv_norm) @ w_kv_up).reshape(bsz, t, H_A, NOPE_A + V_A).transpose(0, 2, 1, 3)
        q_nope, q_rope = qa[..., :NOPE_A], qa[..., NOPE_A:]
        k_nope, va = kva[..., :NOPE_A], kva[..., NOPE_A:]
        k_rope = kr[:, None]
        if ra is not None:
            q_rope = apply_rope(q_rope, ra)
            k_rope = apply_rope(k_rope, ra)
        qa = jnp.concatenate([q_nope, q_rope], axis=-1) * ((NOPE_A + ROPE_A) ** -0.5)
        ka = jnp.concatenate([k_nope, jnp.broadcast_to(k_rope, (bsz, H_A, t, ROPE_A))], axis=-1)
        qb = rmsnorm(gq.reshape(bsz, t, H_B, HEAD_DIM).transpose(0, 2, 1, 3), gq_norm)
        kb = rmsnorm(gk.reshape(bsz, t, KV_B, HEAD_DIM).transpose(0, 2, 1, 3), gk_norm)
        vb = gv.reshape(bsz, t, KV_B, HEAD_DIM).transpose(0, 2, 1, 3)
        if rb is not None:
            qb = apply_rope(qb, rb)
            kb = apply_rope(kb, rb)
        qb = qb.reshape(bsz, KV_B, G_B, t, HEAD_DIM) * (HEAD_DIM ** -0.5)
        return qa[:, :, None], ka, va, qb, kb, vb

    qa_l, ka_l, va_l, qb_l, kb_l, vb_l = project(h_lat, rope_a, rope_b)
    qa_c, ka_c, va_c, qb_c, kb_c, vb_c = project(h_ctx, None, None)
    bsz, s = h_lat.shape[0], h_lat.shape[1]
    oa = prefix_attention(qa_l, ka_l, va_l, ka_c, va_c).reshape(bsz, H_A, s, V_A)
    ob = prefix_attention(qb_l, kb_l, vb_l, kb_c, vb_c).reshape(bsz, H_B, s, HEAD_DIM)
    out_lat = jnp.concatenate([heads_to_tokens(oa), heads_to_tokens(ob)], axis=-1) @ w_out
    out_ctx = None
    if need_ctx:
        t = h_ctx.shape[1]
        oa_c = attend(qa_c, ka_c, va_c).reshape(bsz, H_A, t, V_A)
        ob_c = attend(qb_c, kb_c, vb_c).reshape(bsz, H_B, t, HEAD_DIM)
        out_ctx = jnp.concatenate([heads_to_tokens(oa_c), heads_to_tokens(ob_c)], axis=-1) @ w_out
    return out_ctx, out_lat


def mixer_na(h_ctx, h_lat, nbr_idx, rel_idx, w_in, w_out, rpb, need_ctx):
    def project(h):
        bsz, t, _ = h.shape
        p = (h @ w_in).reshape(bsz, t, 3, H_C, HEAD_DIM).transpose(2, 0, 3, 1, 4)
        return p[0] * (HEAD_DIM ** -0.5), p[1], p[2]

    q_l, k_l, v_l = project(h_lat)
    q_c, k_c, v_c = project(h_ctx)
    bsz, s = h_lat.shape[0], h_lat.shape[1]
    nk = nbr_idx.shape[1]
    nb = s // Q_BLOCK
    rpb_flat = rpb.reshape(H_C, -1)
    qb = q_l.reshape(bsz, H_C, nb, Q_BLOCK, HEAD_DIM).transpose(2, 0, 1, 3, 4)
    idx_b = nbr_idx.reshape(nb, Q_BLOCK, nk)
    rel_b = rel_idx.reshape(nb, Q_BLOCK, nk)

    def one_block(args):
        qblk, ib, rb = args
        kg = jnp.take(k_l, ib, axis=2)
        vg = jnp.take(v_l, ib, axis=2)
        s_n = jnp.einsum('bhqd,bhqnd->bhqn', qblk, kg, preferred_element_type=jnp.float32) + rpb_flat[:, rb].astype(jnp.float32)
        s_c = jnp.einsum('bhqd,bhcd->bhqc', qblk, k_c, preferred_element_type=jnp.float32)
        p = jax.nn.softmax(jnp.concatenate([s_n, s_c], axis=-1), axis=-1).astype(v_l.dtype)
        return jnp.einsum('bhqn,bhqnd->bhqd', p[..., :nk], vg) + jnp.einsum('bhqc,bhcd->bhqd', p[..., nk:], v_c)

    o = lax.map(one_block, (qb, idx_b, rel_b))
    out_lat = o.transpose(1, 0, 3, 2, 4).reshape(bsz, s, C_WIDTH) @ w_out
    out_ctx = None
    if need_ctx:
        oc = attend(q_c[:, :, None], k_c, v_c)[:, :, 0]
        out_ctx = heads_to_tokens(oc) @ w_out
    return out_ctx, out_lat


def hier_moe(h, w_rg, b_rg, w_re, b_re, w_gate, w_up, w_down):
    n, d = h.shape
    hf = h.astype(jnp.float32)
    pg = jax.nn.softmax(hf @ w_rg.astype(jnp.float32) + b_rg.astype(jnp.float32), axis=-1)
    g_sel = jnp.argmax(pg, axis=-1)
    p_gsel = jnp.take_along_axis(pg, g_sel[:, None], axis=-1)
    le = (hf @ w_re.astype(jnp.float32) + b_re.astype(jnp.float32)).reshape(n, N_GROUPS, EXPERTS_PER_GROUP)
    le_sel = jnp.take_along_axis(le, g_sel[:, None, None], axis=1)[:, 0]
    top_p, top_i = lax.top_k(jax.nn.softmax(le_sel, axis=-1), TOP_K)
    gate_w = p_gsel * top_p / jnp.sum(top_p, axis=-1, keepdims=True)
    experts = g_sel[:, None] * EXPERTS_PER_GROUP + top_i

    nkk = n * TOP_K
    flat_e = experts.reshape(-1).astype(jnp.int32)
    flat_w = gate_w.reshape(-1)
    order = jnp.argsort(flat_e)
    sorted_e = flat_e[order]
    tok_sorted = (jnp.arange(nkk, dtype=jnp.int32) // TOP_K)[order]
    w_sorted = flat_w[order]
    counts = jnp.bincount(flat_e, length=N_EXPERTS)
    padded = ((counts + MOE_BLOCK - 1) // MOE_BLOCK) * MOE_BLOCK
    ends = jnp.cumsum(padded)
    pad_start = ends - padded
    start = jnp.cumsum(counts) - counts
    rank = jnp.arange(nkk, dtype=jnp.int32) - start[sorted_e]
    dest = pad_start[sorted_e] + rank
    total = -(-(nkk + N_EXPERTS * (MOE_BLOCK - 1)) // MOE_BLOCK) * MOE_BLOCK
    nblk = total // MOE_BLOCK
    buf_tok = jnp.full((total,), n, jnp.int32).at[dest].set(tok_sorted)
    h_pad = jnp.concatenate([h, jnp.zeros((1, d), h.dtype)], axis=0)
    x_buf = h_pad[buf_tok].reshape(nblk, MOE_BLOCK, d)
    blk_e = jnp.clip(jnp.searchsorted(ends, jnp.arange(nblk, dtype=ends.dtype) * MOE_BLOCK, side='right'), 0, N_EXPERTS - 1)

    def run_block(args):
        xb, e = args
        return (jax.nn.silu(xb @ w_gate[e]) * (xb @ w_up[e])) @ w_down[e]

    y = lax.map(run_block, (x_buf, blk_e)).reshape(total, d)
    return jnp.zeros((n, d), h.dtype).at[tok_sorted].add(y[dest] * w_sorted[:, None].astype(h.dtype))


def setup_inputs(seed: int = 0) -> dict:
    key = jax.random.key(seed)
    ks = jax.random.split(key, 32)

    def nrm(k, shape, scale):
        return jax.random.normal(k, shape, jnp.float32) * scale

    def gain(k, shape):
        return 1.0 + nrm(k, shape, 0.1)

    D = D_MODEL
    return {
        "x": nrm(ks[0], (BATCH, SEQ, D), 1.0),
        "c": nrm(ks[1], (BATCH, D), 1.0),
        "ctx": nrm(ks[2], (BATCH, CTX_LEN, D), 1.0),
        "c_ctx": nrm(ks[3], (D,), 1.0),
        "w_ada": nrm(ks[4], (DEPTH, D, 6 * D), 0.02),
        "b_ada": nrm(ks[5], (DEPTH, 6 * D), 0.02),
        "norm_mix": gain(ks[6], (DEPTH, D)),
        "norm_ffn": gain(ks[7], (DEPTH, D)),
        "norm_final": gain(ks[8], (D,)),
        "ab_w_in": nrm(ks[9], (N_EVEN, D, AB_IN), D ** -0.5),
        "ab_w_out": nrm(ks[10], (N_EVEN, AB_OUT, D), AB_OUT ** -0.5),
        "mla_q_norm": gain(ks[11], (N_EVEN, Q_LORA)),
        "mla_w_q_up": nrm(ks[12], (N_EVEN, Q_LORA, H_A * (NOPE_A + ROPE_A)), Q_LORA ** -0.5),
        "mla_kv_norm": gain(ks[13], (N_EVEN, KV_LORA)),
        "mla_w_kv_up": nrm(ks[14], (N_EVEN, KV_LORA, H_A * (NOPE_A + V_A)), KV_LORA ** -0.5),
        "gqa_q_norm": gain(ks[15], (N_EVEN, HEAD_DIM)),
        "gqa_k_norm": gain(ks[16], (N_EVEN, HEAD_DIM)),
        "na_w_in": nrm(ks[17], (N_ODD, D, 3 * C_WIDTH), D ** -0.5),
        "na_w_out": nrm(ks[18], (N_ODD, C_WIDTH, D), C_WIDTH ** -0.5),
        "na_rpb": nrm(ks[19], (N_ODD, H_C, 2 * WIN_R - 1, 2 * WIN_C - 1), 0.1),
        "moe_w_rg": nrm(ks[20], (DEPTH, D, N_GROUPS), D ** -0.5),
        "moe_b_rg": nrm(ks[21], (DEPTH, N_GROUPS), 0.01),
        "moe_w_re": nrm(ks[22], (DEPTH, D, N_EXPERTS), D ** -0.5),
        "moe_b_re": nrm(ks[23], (DEPTH, N_EXPERTS), 0.01),
        "moe_w_gate": nrm(ks[24], (DEPTH, N_EXPERTS, D, D_EXPERT), D ** -0.5),
        "moe_w_up": nrm(ks[25], (DEPTH, N_EXPERTS, D, D_EXPERT), D ** -0.5),
        "moe_w_down": nrm(ks[26], (DEPTH, N_EXPERTS, D_EXPERT, D), D_EXPERT ** -0.5),
    }


def reference(x, c, ctx, c_ctx, w_ada, b_ada, norm_mix, norm_ffn, norm_final, ab_w_in, ab_w_out, mla_q_norm, mla_w_q_up, mla_kv_norm, mla_w_kv_up, gqa_q_norm, gqa_k_norm, na_w_in, na_w_out, na_rpb, moe_w_rg, moe_b_rg, moe_w_re, moe_b_re, moe_w_gate, moe_w_up, moe_w_down):
    bsz, s, d = x.shape
    n_ctx = ctx.shape[1]
    rows_n = s // GRID_W
    wr = min(WIN_R, rows_n)
    t = jnp.arange(s, dtype=jnp.int32)
    r = t // GRID_W
    col = t % GRID_W
    rope_a = axial_rope_table(r.astype(jnp.float32), col.astype(jnp.float32), ROPE_A)
    rope_b = axial_rope_table(r.astype(jnp.float32), col.astype(jnp.float32), HEAD_DIM)
    rs = jnp.clip(r - wr // 2, 0, rows_n - wr)
    cs = jnp.clip(col - WIN_C // 2, 0, GRID_W - WIN_C)
    key_r = rs[:, None] + jnp.arange(wr, dtype=jnp.int32)[None]
    key_c = cs[:, None] + jnp.arange(WIN_C, dtype=jnp.int32)[None]
    nbr_idx = (key_r[:, :, None] * GRID_W + key_c[:, None, :]).reshape(s, wr * WIN_C)
    rel_idx = ((key_r - r[:, None] + WIN_R - 1)[:, :, None] * (2 * WIN_C - 1)
               + (key_c - col[:, None] + WIN_C - 1)[:, None, :]).reshape(s, wr * WIN_C)

    x_lat = x
    x_ctx = ctx
    silu_c = jax.nn.silu(c)
    silu_cc = jax.nn.silu(c_ctx)
    for l in range(DEPTH):
        last = l == DEPTH - 1
        mod_l = (silu_c @ w_ada[l] + b_ada[l])[:, None, :]
        mod_c = (silu_cc @ w_ada[l] + b_ada[l])[None, None, :]
        sh1, sc1, g1, sh2, sc2, g2 = jnp.split(mod_l, 6, axis=-1)
        sh1c, sc1c, g1c, sh2c, sc2c, g2c = jnp.split(mod_c, 6, axis=-1)
        h_l = modulate(x_lat, norm_mix[l], sh1, sc1)
        h_c = modulate(x_ctx, norm_mix[l], sh1c, sc1c)
        if l % 2 == 0:
            i = l // 2
            o_c, o_l = mixer_ab(h_c, h_l, rope_a, rope_b, ab_w_in[i], ab_w_out[i], mla_q_norm[i], mla_w_q_up[i],
                                mla_kv_norm[i], mla_w_kv_up[i], gqa_q_norm[i], gqa_k_norm[i], not last)
        else:
            i = l // 2
            o_c, o_l = mixer_na(h_c, h_l, nbr_idx, rel_idx, na_w_in[i], na_w_out[i], na_rpb[i], not last)
        x_lat = x_lat + g1 * o_l
        h_l = modulate(x_lat, norm_ffn[l], sh2, sc2)
        if not last:
            x_ctx = x_ctx + g1c * o_c
            h_c = modulate(x_ctx, norm_ffn[l], sh2c, sc2c)
            tokens = jnp.concatenate([h_c.reshape(-1, d), h_l.reshape(-1, d)], axis=0)
            y = hier_moe(tokens, moe_w_rg[l], moe_b_rg[l], moe_w_re[l], moe_b_re[l], moe_w_gate[l], moe_w_up[l], moe_w_down[l])
            n_c = bsz * n_ctx
            x_ctx = x_ctx + g2c * y[:n_c].reshape(bsz, n_ctx, d)
            x_lat = x_lat + g2 * y[n_c:].reshape(bsz, s, d)
        else:
            y = hier_moe(h_l.reshape(-1, d), moe_w_rg[l], moe_b_rg[l], moe_w_re[l], moe_b_re[l], moe_w_gate[l], moe_w_up[l], moe_w_down[l])
            x_lat = x_lat + g2 * y.reshape(bsz, s, d)
    return rmsnorm(x_lat, norm_final)
```

```python
import functools

import numpy as np
import jax
import jax.numpy as jnp
from jax import lax
from jax.experimental import pallas as pl
from jax.experimental.pallas import tpu as pltpu

F32 = jnp.float32
BF16 = jnp.bfloat16

GRID_W = 64
HEAD_DIM = 64
ROPE_THETA = 10000.0
RMS_EPS = 1e-6
H_A = 8
NOPE_A = 64
ROPE_A = 32
V_A = 64
Q_LORA = 384
KV_LORA = 256
H_B = 8
KV_B = 2
G_B = H_B // KV_B
H_C = 16
WIN_R = 8
WIN_C = 16
N_GROUPS = 4
EXPERTS_PER_GROUP = 8
N_EXPERTS = N_GROUPS * EXPERTS_PER_GROUP
D_EXPERT = 512

LANE = 128
TM = 256
MOE_BM = 256
NA_ROWS = TM // GRID_W
NEG = -1e30
VMEM_LIMIT = 48 * 1024 * 1024

_C_CQ = 0
_C_CKV = _C_CQ + Q_LORA
_C_KR = _C_CKV + KV_LORA
_C_GQ = _C_KR + LANE
_C_GK = _C_GQ + H_B * LANE
_C_GV = _C_GK + KV_B * HEAD_DIM
_C_END = _C_GV + KV_B * HEAD_DIM


def _cparams(n_axes, vmem=VMEM_LIMIT):
    return pltpu.CompilerParams(dimension_semantics=("arbitrary",) * n_axes, vmem_limit_bytes=vmem)


def _rms(x):
    return x * lax.rsqrt(jnp.mean(x * x, axis=-1, keepdims=True) + RMS_EPS)


def _modulate(x, g, shift, scale):
    return (_rms(x) * g) * (1.0 + scale) + shift


def _silu(x):
    return x / (1.0 + jnp.exp(-x))


def _dot(a, b):
    return jnp.dot(a, b, preferred_element_type=F32)


def _dot_nt(a, b):
    return lax.dot_general(a, b, (((1,), (1,)), ((), ())), preferred_element_type=F32)


def _ada_kernel(c_ref, w_ref, b_ref, o_ref, *, n_vec):
    s = _silu(c_ref[...])
    w = w_ref[0]
    rows = []
    for r in range(8):
        if r < n_vec:
            rows.append(jnp.sum(w * s[:, r:r + 1], axis=0, keepdims=True) + b_ref[0])
        else:
            rows.append(jnp.zeros_like(b_ref[0]))
    o_ref[0] = jnp.concatenate(rows, axis=0)


def _ada_mod(cond_t, w_ada, b_ada, n_vec):
    depth, d, n6 = w_ada.shape
    tn = 512
    return pl.pallas_call(
        functools.partial(_ada_kernel, n_vec=n_vec),
        out_shape=jax.ShapeDtypeStruct((depth, 8, n6), F32),
        grid=(depth, n6 // tn),
        in_specs=[
            pl.BlockSpec((d, 8), lambda l, j: (0, 0)),
            pl.BlockSpec((1, d, tn), lambda l, j: (l, 0, j)),
            pl.BlockSpec((1, 1, tn), lambda l, j: (l, 0, j)),
        ],
        out_specs=pl.BlockSpec((1, 8, tn), lambda l, j: (l, 0, j)),
        compiler_params=_cparams(2),
    )(cond_t, w_ada, b_ada.reshape(depth, 1, n6))


def _rope(x, tab, sh):
    return (x * tab[:, 0:LANE]
            + pltpu.roll(x, LANE - sh, 1) * tab[:, LANE:2 * LANE]
            + pltpu.roll(x, sh, 1) * tab[:, 2 * LANE:3 * LANE])


def _ab_proj_kernel(x_ref, mod_ref, g_ref, win_ref, qn_ref, wq_ref, kvn_ref, wk_ref, wv_ref,
                    gqn_ref, gkn_ref, ta_ref, tb_ref,
                    qa_ref, ka_ref, va_ref, qb_ref, kb_ref, vb_ref):
    mod = mod_ref[0]
    h = _modulate(x_ref[0], g_ref[...], mod[0:1], mod[1:2]).astype(BF16)
    p = _dot(h, win_ref[...])
    ta = ta_ref[...]
    tb = tb_ref[...]
    scale_a = (NOPE_A + ROPE_A) ** -0.5
    scale_b = HEAD_DIM ** -0.5

    cqn = (_rms(p[:, _C_CQ:_C_CKV]) * qn_ref[...]).astype(BF16)
    qa = _dot(cqn, wq_ref[...])
    ckvn = (_rms(p[:, _C_CKV:_C_KR]) * kvn_ref[...]).astype(BF16)
    ka = _dot(ckvn, wk_ref[...])
    va_ref[0] = _dot(ckvn, wv_ref[...]).astype(BF16)
    kr = _rope(p[:, _C_KR:_C_GQ], ta, ROPE_A // 2)
    for hh in range(H_A):
        sl = slice(hh * LANE, (hh + 1) * LANE)
        qa_ref[0, :, sl] = (_rope(qa[:, sl], ta, ROPE_A // 2) * scale_a).astype(BF16)
        ka_ref[0, :, sl] = (ka[:, sl] + kr).astype(BF16)

    for hh in range(H_B):
        sl = slice(hh * LANE, (hh + 1) * LANE)
        xh = p[:, _C_GQ + hh * LANE:_C_GQ + (hh + 1) * LANE]
        r = lax.rsqrt(jnp.sum(xh * xh, axis=-1, keepdims=True) * (1.0 / HEAD_DIM) + RMS_EPS)
        qb_ref[0, :, sl] = (_rope(xh * r * gqn_ref[:, sl], tb, HEAD_DIM // 2) * scale_b).astype(BF16)
    gk = p[:, _C_GK:_C_GV]
    lo = lax.broadcasted_iota(jnp.int32, gk.shape, 1) < HEAD_DIM
    sq = gk * gk
    s0 = jnp.sum(jnp.where(lo, sq, 0.0), axis=-1, keepdims=True)
    s1 = jnp.sum(jnp.where(lo, 0.0, sq), axis=-1, keepdims=True)
    r = jnp.where(lo, lax.rsqrt(s0 * (1.0 / HEAD_DIM) + RMS_EPS), lax.rsqrt(s1 * (1.0 / HEAD_DIM) + RMS_EPS))
    kb_ref[0] = _rope(gk * r * gkn_ref[...], tb, HEAD_DIM // 2).astype(BF16)
    vb_ref[0] = p[:, _C_GV:_C_END].astype(BF16)


def _ab_proj(xall, mod3, layer, g, w, tabs, n_lat_tiles, bsz):
    _, t, d = xall.shape
    nt = t // TM
    const = lambda b, i: (0, 0)
    tok = lambda b, i: (b, i, 0)

    def mod_map(b, i):
        return (layer * 8 + jnp.where(i >= n_lat_tiles, bsz, b), 0, 0)

    def full(a):
        return pl.BlockSpec(a.shape, const)

    outs = [H_A * LANE, H_A * LANE, H_A * V_A, H_B * LANE, KV_B * HEAD_DIM, KV_B * HEAD_DIM]
    return pl.pallas_call(
        _ab_proj_kernel,
        out_shape=[jax.ShapeDtypeStruct((bsz, t, n), BF16) for n in outs],
        grid=(bsz, nt),
        in_specs=[
            pl.BlockSpec((1, TM, d), tok),
            pl.BlockSpec((1, 6, d), mod_map),
            full(g), full(w["win"]), full(w["qn"]), full(w["wq"]), full(w["kvn"]), full(w["wk"]),
            full(w["wv"]), full(w["gqn"]), full(w["gkn"]),
            pl.BlockSpec((TM, 3 * LANE), lambda b, i: (i, 0)),
            pl.BlockSpec((TM, 3 * LANE), lambda b, i: (i, 0)),
        ],
        out_specs=[pl.BlockSpec((1, TM, n), tok) for n in outs],
        compiler_params=_cparams(2),
    )(xall, mod3, g, w["win"], w["qn"], w["wq"], w["kvn"], w["wk"], w["wv"], w["gqn"], w["gkn"],
      tabs[0], tabs[1])


def _flash_kernel(q_ref, k_ref, v_ref, o_ref, *, n_lat_tiles, s_len, ctx_len, tk, k_per_head):
    i = pl.program_id(2)
    n_chunks = jnp.where(i >= n_lat_tiles, 0, s_len // tk)
    outs = []
    for a in range(2):
        q = q_ref[0, :, a * LANE:(a + 1) * LANE]
        kc0 = a * LANE if k_per_head else 0

        def step(start, size, carry, q=q, kc0=kc0):
            m, l, acc = carry
            kblk = k_ref[0, pl.ds(start, size), kc0:kc0 + LANE]
            vblk = v_ref[0, pl.ds(start, size), :]
            s = _dot_nt(q, kblk)
            m_new = jnp.maximum(m, jnp.max(s, axis=-1, keepdims=True))
            alpha = jnp.exp(m - m_new)
            pexp = jnp.exp(s - m_new)
            l_new = alpha * l + jnp.sum(pexp, axis=-1, keepdims=True)
            acc_new = alpha * acc + _dot(pexp.astype(BF16), vblk)
            return m_new, l_new, acc_new

        carry = (jnp.full((TM, 1), NEG, F32), jnp.zeros((TM, 1), F32), jnp.zeros((TM, LANE), F32))
        carry = step(s_len, ctx_len, carry)
        carry = lax.fori_loop(
            0, n_chunks, lambda j, c, step=step: step(pl.multiple_of(j * tk, tk), tk, c), carry)
        _, l, acc = carry
        outs.append(acc / l)
    lo = lax.broadcasted_iota(jnp.int32, (TM, LANE), 1) < HEAD_DIM
    if k_per_head:
        o = jnp.where(lo, outs[0], outs[1])
    else:
        kv_lo = (pl.program_id(1) // (G_B // 2)) == 0
        o0 = jnp.where(kv_lo, outs[0], pltpu.roll(outs[0], HEAD_DIM, 1))
        o1 = jnp.where(kv_lo, pltpu.roll(outs[1], HEAD_DIM, 1), outs[1])
        o = jnp.where(lo, o0, o1)
    o_ref[0] = o.astype(BF16)


def _flash(q, k, v, s_len, k_per_head):
    bsz, t, qw = q.shape
    n_pairs = qw // (2 * LANE)
    ctx_len = t - s_len
    nt = t // TM
    tk = min(1024, s_len)
    kw = 2 * LANE if k_per_head else LANE
    pair_or_zero = (lambda b, p, i: (b, 0, p)) if k_per_head else (lambda b, p, i: (b, 0, 0))
    return pl.pallas_call(
        functools.partial(_flash_kernel, n_lat_tiles=s_len // TM, s_len=s_len, ctx_len=ctx_len, tk=tk,
                          k_per_head=k_per_head),
        out_shape=jax.ShapeDtypeStruct((bsz, t, n_pairs * LANE), BF16),
        grid=(bsz, n_pairs, nt),
        in_specs=[
            pl.BlockSpec((1, TM, 2 * LANE), lambda b, p, i: (b, i, p)),
            pl.BlockSpec((1, t, kw), pair_or_zero),
            pl.BlockSpec((1, t, LANE), pair_or_zero),
        ],
        out_specs=pl.BlockSpec((1, TM, LANE), lambda b, p, i: (b, i, p)),
        compiler_params=_cparams(3),
    )(q, k, v)


def _na_proj_kernel(x_ref, mod_ref, g_ref, w_ref, q_ref, k_ref, v_ref):
    mod = mod_ref[0]
    h = _modulate(x_ref[0], g_ref[...], mod[0:1], mod[1:2]).astype(BF16)
    p = _dot(h, w_ref[...])
    cw = H_C * HEAD_DIM
    q_ref[0] = (p[:, 0:cw] * (HEAD_DIM ** -0.5)).astype(BF16)
    k_ref[0] = p[:, cw:2 * cw].astype(BF16)
    v_ref[0] = p[:, 2 * cw:3 * cw].astype(BF16)


def _na_proj(xall, mod3, layer, g, w, n_lat_tiles, bsz):
    _, t, d = xall.shape
    cw = H_C * HEAD_DIM
    tok = lambda b, i: (b, i, 0)
    return pl.pallas_call(
        _na_proj_kernel,
        out_shape=[jax.ShapeDtypeStruct((bsz, t, cw), BF16)] * 3,
        grid=(bsz, t // TM),
        in_specs=[
            pl.BlockSpec((1, TM, d), tok),
            pl.BlockSpec((1, 6, d), lambda b, i: (layer * 8 + jnp.where(i >= n_lat_tiles, bsz, b), 0, 0)),
            pl.BlockSpec(g.shape, lambda b, i: (0, 0)),
            pl.BlockSpec(w.shape, lambda b, i: (0, 0)),
        ],
        out_specs=[pl.BlockSpec((1, TM, cw), tok)] * 3,
        compiler_params=_cparams(2),
    )(xall, mod3, g, w)


def _na_kernel(q_ref, k_ref, v_ref, bias_ref, o_ref, *, n_lat_tiles, s_len, ctx_len, rows_n):
    i = pl.program_id(2)
    nwin = WIN_R * GRID_W
    kc = k_ref[0, s_len:s_len + ctx_len, :]
    vc = v_ref[0, s_len:s_len + ctx_len, :]

    @pl.when(i < n_lat_tiles)
    def _():
        lo = lax.broadcasted_iota(jnp.int32, (GRID_W, LANE), 1) < HEAD_DIM
        for rr in range(NA_ROWS):
            r = i * NA_ROWS + rr
            rs = jnp.clip(r - WIN_R // 2, 0, rows_n - WIN_R)
            var = r - rs
            start = pl.multiple_of(rs * GRID_W, GRID_W)
            kwin = k_ref[0, pl.ds(start, nwin), :]
            vwin = v_ref[0, pl.ds(start, nwin), :]
            q2 = q_ref[0, rr * GRID_W:(rr + 1) * GRID_W, :]
            outs = []
            for a in range(2):
                qa = jnp.where(lo if a == 0 else jnp.logical_not(lo), q2, jnp.zeros_like(q2))
                sn = _dot_nt(qa, kwin) + bias_ref[var, a]
                sc = _dot_nt(qa, kc)
                m = jnp.maximum(jnp.max(sn, axis=-1, keepdims=True), jnp.max(sc, axis=-1, keepdims=True))
                pn = jnp.exp(sn - m)
                pc = jnp.exp(sc - m)
                l = jnp.sum(pn, axis=-1, keepdims=True) + jnp.sum(pc, axis=-1, keepdims=True)
                outs.append((_dot(pn.astype(BF16), vwin) + _dot(pc.astype(BF16), vc)) / l)
            o_ref[0, rr * GRID_W:(rr + 1) * GRID_W, :] = jnp.where(lo, outs[0], outs[1]).astype(BF16)

    @pl.when(i >= n_lat_tiles)
    def _():
        lo = lax.broadcasted_iota(jnp.int32, (TM, LANE), 1) < HEAD_DIM
        q2 = q_ref[0]
        outs = []
        for a in range(2):
            qa = jnp.where(lo if a == 0 else jnp.logical_not(lo), q2, jnp.zeros_like(q2))
            sc = _dot_nt(qa, kc)
            m = jnp.max(sc, axis=-1, keepdims=True)
            pc = jnp.exp(sc - m)
            l = jnp.sum(pc, axis=-1, keepdims=True)
            outs.append(_dot(pc.astype(BF16), vc) / l)
        o_ref[0] = jnp.where(lo, outs[0], outs[1]).astype(BF16)


def _na_attention(q, k, v, bias, s_len):
    bsz, t, cw = q.shape
    n_pairs = cw // LANE
    rows_n = s_len // GRID_W
    assert rows_n >= WIN_R and GRID_W >= WIN_C
    tile = lambda b, p, i: (b, i, p)
    res = lambda b, p, i: (b, 0, p)
    return pl.pallas_call(
        functools.partial(_na_kernel, n_lat_tiles=s_len // TM, s_len=s_len, ctx_len=t - s_len, rows_n=rows_n),
        out_shape=jax.ShapeDtypeStruct((bsz, t, cw), BF16),
        grid=(bsz, n_pairs, t // TM),
        in_specs=[
            pl.BlockSpec((1, TM, LANE), tile),
            pl.BlockSpec((1, t, LANE), res),
            pl.BlockSpec((1, t, LANE), res),
            pl.BlockSpec((WIN_R, 2, GRID_W, WIN_R * GRID_W), lambda b, p, i: (0, p, 0, 0)),
        ],
        out_specs=pl.BlockSpec((1, TM, LANE), tile),
        compiler_params=_cparams(3),
    )(q, k, v, bias)


def _na_bias_table(rpb):
    var = np.arange(WIN_R)[:, None, None, None]
    j = np.arange(WIN_R)[None, :, None, None]
    col = np.arange(GRID_W)[None, None, :, None]
    kc = np.arange(GRID_W)[None, None, None, :]
    cs = np.clip(col - WIN_C // 2, 0, GRID_W - WIN_C)
    valid = np.broadcast_to((kc >= cs) & (kc < cs + WIN_C), (WIN_R, WIN_R, GRID_W, GRID_W))
    ri = np.broadcast_to(j - var + WIN_R - 1, valid.shape)
    ci = np.broadcast_to(np.clip(kc - col + WIN_C - 1, 0, 2 * WIN_C - 2), valid.shape)
    vals = rpb.astype(F32)[:, ri, ci]
    bias = jnp.where(valid[None], vals, NEG)
    return bias.transpose(1, 0, 3, 2, 4).reshape(WIN_R, rpb.shape[0], GRID_W, WIN_R * GRID_W)


def _post_kernel(x_ref, o1_ref, o2_ref, wo_ref, mod_ref, g_ref, wr_ref, br_ref, tri_ref,
                 xo_ref, route_ref, cnt_ref, base_ref):
    @pl.when((pl.program_id(0) == 0) & (pl.program_id(1) == 0))
    def _():
        base_ref[...] = jnp.zeros_like(base_ref)

    mod = mod_ref[0]
    half = wo_ref.shape[0] // 2
    o = _dot(o1_ref[0], wo_ref[0:half, :]) + _dot(o2_ref[0], wo_ref[half:2 * half, :])
    xn = x_ref[0] + mod[2:3] * o
    xo_ref[0] = xn

    h2 = _modulate(xn, g_ref[...], mod[3:4], mod[4:5])
    logits = jnp.dot(h2, wr_ref[...], preferred_element_type=F32, precision=lax.Precision.HIGHEST) + br_ref[...]
    lane = lax.broadcasted_iota(jnp.int32, logits.shape, 1)
    isg = (lane >= N_EXPERTS) & (lane < N_EXPERTS + N_GROUPS)
    lg = jnp.where(isg, logits, NEG)
    eg = jnp.where(isg, jnp.exp(lg - jnp.max(lg, axis=-1, keepdims=True)), 0.0)
    pg = eg / jnp.sum(eg, axis=-1, keepdims=True)
    p_gsel = jnp.max(pg, axis=-1, keepdims=True)
    gsel = jnp.min(jnp.where(isg & (pg == p_gsel), lane - N_EXPERTS, N_GROUPS), axis=-1, keepdims=True)
    sel = (lane < N_EXPERTS) & ((lane // EXPERTS_PER_GROUP) == gsel)
    le = jnp.where(sel, logits, NEG)
    ee = jnp.where(sel, jnp.exp(le - jnp.max(le, axis=-1, keepdims=True)), 0.0)
    pe = ee / jnp.sum(ee, axis=-1, keepdims=True)
    pm = jnp.where(sel, pe, -1.0)
    p1 = jnp.max(pm, axis=-1, keepdims=True)
    i1 = jnp.min(jnp.where(pm == p1, lane, LANE), axis=-1, keepdims=True)
    pm2 = jnp.where(lane == i1, -2.0, pm)
    p2 = jnp.max(pm2, axis=-1, keepdims=True)
    i2 = jnp.min(jnp.where(pm2 == p2, lane, LANE), axis=-1, keepdims=True)
    den = p1 + p2
    w1 = p_gsel * p1 / den
    w2 = p_gsel * p2 / den

    half_l = LANE // 2
    oh1 = lane == i1
    oh2 = lane == i2 + half_l
    oh = jnp.where(oh1 | oh2, 1.0, 0.0)
    oh_sw = jnp.where((lane == i2) | (lane == i1 + half_l), 1.0, 0.0)
    prefix = _dot(tri_ref[...], oh.astype(BF16))
    cnt = jnp.sum(oh, axis=0, keepdims=True)
    cnt_sw = jnp.sum(oh_sw, axis=0, keepdims=True)
    base = base_ref[...]
    lane1 = lax.broadcasted_iota(jnp.int32, base.shape, 1)
    tot = prefix + base + jnp.where(lane1 >= half_l, cnt_sw, 0.0)
    r1 = jnp.sum(jnp.where(oh1, tot, 0.0), axis=-1, keepdims=True)
    r2 = jnp.sum(jnp.where(oh2, tot, 0.0), axis=-1, keepdims=True)
    new_base = base + cnt + cnt_sw
    base_ref[...] = new_base
    cnt_ref[...] = new_base

    cols = [i1.astype(F32), i2.astype(F32), r1, r2, w1, w2]
    route = jnp.zeros(logits.shape, F32)
    for c, val in enumerate(cols):
        route = jnp.where(lane == c, val, route)
    route_ref[0] = route


def _post(xall, o1, o2, o1_col, o2_col, wo, mod3, layer, g, wr, br, tri, n_lat_tiles, bsz):
    _, t, d = xall.shape
    half = wo.shape[0] // 2
    tok = lambda b, i: (b, i, 0)
    const = lambda b, i: (0, 0)
    return pl.pallas_call(
        _post_kernel,
        out_shape=[jax.ShapeDtypeStruct((bsz, t, d), F32), jax.ShapeDtypeStruct((bsz, t, LANE), F32),
                   jax.ShapeDtypeStruct((1, LANE), F32)],
        grid=(bsz, t // TM),
        in_specs=[
            pl.BlockSpec((1, TM, d), tok),
            pl.BlockSpec((1, TM, half), lambda b, i: (b, i, o1_col)),
            pl.BlockSpec((1, TM, half), lambda b, i: (b, i, o2_col)),
            pl.BlockSpec(wo.shape, const),
            pl.BlockSpec((1, 6, d), lambda b, i: (layer * 8 + jnp.where(i >= n_lat_tiles, bsz, b), 0, 0)),
            pl.BlockSpec(g.shape, const),
            pl.BlockSpec(wr.shape, const),
            pl.BlockSpec(br.shape, const),
            pl.BlockSpec(tri.shape, const),
        ],
        out_specs=[pl.BlockSpec((1, TM, d), tok), pl.BlockSpec((1, TM, LANE), tok),
                   pl.BlockSpec((1, LANE), const)],
        scratch_shapes=[pltpu.VMEM((1, LANE), F32)],
        compiler_params=_cparams(2),
    )(xall, o1, o2, wo, mod3, g, wr, br, tri)


def _dispatch_kernel(dest_ref, x_ref, mod_ref, g_ref, xs_in_ref, xs_ref, hbuf, sem):
    del xs_in_ref
    mod = mod_ref[0]
    hbuf[...] = _modulate(x_ref[0], g_ref[...], mod[3:4], mod[4:5])

    def row_copy(j, dst):
        return pltpu.make_async_copy(hbuf.at[pl.ds(j, 1)], xs_ref.at[pl.ds(dst, 1)], sem)

    def issue(j, c):
        for k in range(2):
            row_copy(j, dest_ref[0, 0, k * TM + j]).start()
        return c

    lax.fori_loop(0, TM, issue, 0)

    def drain(j, c):
        for k in range(2):
            row_copy(j, dest_ref[0, 0, k * TM + j]).wait()
        return c

    lax.fori_loop(0, TM, drain, 0)


def _dispatch(dest, xall, mod3, layer, g, total, n_lat_tiles, bsz):
    _, t, d = xall.shape
    nt = t // TM
    xs0 = jnp.zeros((total, d), F32)
    return pl.pallas_call(
        _dispatch_kernel,
        out_shape=jax.ShapeDtypeStruct((total, d), F32),
        grid=(bsz, nt),
        in_specs=[
            pl.BlockSpec((1, 1, 2 * TM), lambda b, i: (b * nt + i, 0, 0), memory_space=pltpu.SMEM),
            pl.BlockSpec((1, TM, d), lambda b, i: (b, i, 0)),
            pl.BlockSpec((1, 6, d), lambda b, i: (layer * 8 + jnp.where(i >= n_lat_tiles, bsz, b), 0, 0)),
            pl.BlockSpec(g.shape, lambda b, i: (0, 0)),
            pl.BlockSpec(memory_space=pl.ANY),
        ],
        out_specs=pl.BlockSpec(memory_space=pl.ANY),
        scratch_shapes=[pltpu.VMEM((TM, d), F32), pltpu.SemaphoreType.DMA(())],
        input_output_aliases={4: 0},
        compiler_params=_cparams(2),
    )(dest, xall, mod3, g, xs0)


def _expert_kernel(be_ref, nb_ref, xs_ref, wg_ref, wu_ref, wd_ref, ys_ref, wgb, wub, wdb):
    i = pl.program_id(0)
    changed = (i == 0) | (be_ref[i] != be_ref[jnp.maximum(i - 1, 0)])

    @pl.when(changed)
    def _():
        wgb[...] = wg_ref[0, 0].astype(BF16)
        wub[...] = wu_ref[0, 0].astype(BF16)
        wdb[...] = wd_ref[0, 0].astype(BF16)

    @pl.when(i < nb_ref[0])
    def _():
        x = xs_ref[...].astype(BF16)
        a = _silu(_dot(x, wgb[...])) * _dot(x, wub[...])
        ys_ref[...] = _dot(a.astype(BF16), wdb[...])

    @pl.when(i >= nb_ref[0])
    def _():
        ys_ref[...] = jnp.zeros_like(ys_ref)


def _experts(blk_e, nb_used, xs, w_gate, w_up, w_down, layer):
    total, d = xs.shape
    de = w_gate.shape[-1]
    nblk = total // MOE_BM
    return pl.pallas_call(
        _expert_kernel,
        out_shape=jax.ShapeDtypeStruct((total, d), F32),
        grid_spec=pltpu.PrefetchScalarGridSpec(
            num_scalar_prefetch=2,
            grid=(nblk,),
            in_specs=[
                pl.BlockSpec((MOE_BM, d), lambda i, be, nb: (i, 0)),
                pl.BlockSpec((1, 1, d, de), lambda i, be, nb: (layer, be[i], 0, 0)),
                pl.BlockSpec((1, 1, d, de), lambda i, be, nb: (layer, be[i], 0, 0)),
                pl.BlockSpec((1, 1, de, d), lambda i, be, nb: (layer, be[i], 0, 0)),
            ],
            out_specs=pl.BlockSpec((MOE_BM, d), lambda i, be, nb: (i, 0)),
            scratch_shapes=[pltpu.VMEM((d, de), BF16), pltpu.VMEM((d, de), BF16), pltpu.VMEM((de, d), BF16)],
        ),
        compiler_params=_cparams(1),
    )(blk_e, nb_used, xs, w_gate, w_up, w_down)


def _combine_kernel(dest_ref, x_ref, route_ref, mod_ref, gf_ref, ys_ref, o_ref, ybuf, sem, *, final):
    def row_copy(j, k, src):
        return pltpu.make_async_copy(ys_ref.at[pl.ds(src, 1)], ybuf.at[pl.ds(k * TM + j, 1)], sem)

    def issue(j, c):
        for k in range(2):
            row_copy(j, k, dest_ref[0, 0, k * TM + j]).start()
        return c

    lax.fori_loop(0, TM, issue, 0)

    def drain(j, c):
        for k in range(2):
            row_copy(j, k, dest_ref[0, 0, k * TM + j]).wait()
        return c

    lax.fori_loop(0, TM, drain, 0)

    route = route_ref[0]
    y = ybuf[0:TM, :] * route[:, 4:5] + ybuf[TM:2 * TM, :] * route[:, 5:6]
    xn = x_ref[0] + mod_ref[0][5:6] * y
    if final:
        xn = _rms(xn) * gf_ref[...]
    o_ref[0] = xn


def _combine(dest, xall, route, mod3, layer, gf, ys, t_out, final, n_lat_tiles, bsz):
    _, t, d = xall.shape
    nt = t // TM
    tok = lambda b, i: (b, i, 0)
    return pl.pallas_call(
        functools.partial(_combine_kernel, final=final),
        out_shape=jax.ShapeDtypeStruct((bsz, t_out, d), F32),
        grid=(bsz, t_out // TM),
        in_specs=[
            pl.BlockSpec((1, 1, 2 * TM), lambda b, i: (b * nt + i, 0, 0), memory_space=pltpu.SMEM),
            pl.BlockSpec((1, TM, d), tok),
            pl.BlockSpec((1, TM, LANE), tok),
            pl.BlockSpec((1, 6, d), lambda b, i: (layer * 8 + jnp.where(i >= n_lat_tiles, bsz, b), 0, 0)),
            pl.BlockSpec(gf.shape, lambda b, i: (0, 0)),
            pl.BlockSpec(memory_space=pl.ANY),
        ],
        out_specs=pl.BlockSpec((1, TM, d), tok),
        scratch_shapes=[pltpu.VMEM((2 * TM, d), F32), pltpu.SemaphoreType.DMA(())],
        compiler_params=_cparams(2),
    )(dest, xall, route, mod3, gf, ys)


def _deint(n):
    return np.concatenate([np.arange(0, n, 2), np.arange(1, n, 2)])


def _rope_tables(s_len, ctx_len):
    tpos = jnp.arange(s_len, dtype=jnp.int32)
    r = (tpos // GRID_W).astype(F32)
    col = (tpos % GRID_W).astype(F32)

    def cos_sin(rot):
        n = rot // 4
        inv = ROPE_THETA ** (-jnp.arange(n, dtype=F32) / n)
        ang = jnp.concatenate([r[:, None] * inv, col[:, None] * inv], axis=-1)
        return jnp.cos(ang), jnp.sin(ang)

    def finish(cos_t, sinm_t, sinp_t, cos_ctx):
        zc = jnp.zeros((ctx_len, LANE), F32)
        return jnp.concatenate([
            jnp.concatenate([cos_t, sinm_t, sinp_t], axis=1),
            jnp.concatenate([cos_ctx, zc, zc], axis=1)], axis=0)

    ca, sa = cos_sin(ROPE_A)
    ha = ROPE_A // 2
    one = jnp.ones((s_len, NOPE_A), F32)
    z = lambda n: jnp.zeros((s_len, n), F32)
    tab_a = finish(
        jnp.concatenate([one, ca, ca, z(LANE - NOPE_A - ROPE_A)], axis=1),
        jnp.concatenate([z(NOPE_A), -sa, z(LANE - NOPE_A - ha)], axis=1),
        jnp.concatenate([z(NOPE_A + ha), sa, z(LANE - NOPE_A - ROPE_A)], axis=1),
        jnp.concatenate([jnp.ones((ctx_len, NOPE_A + ROPE_A), F32),
                         jnp.zeros((ctx_len, LANE - NOPE_A - ROPE_A), F32)], axis=1))
    cb, sb = cos_sin(HEAD_DIM)
    hb = HEAD_DIM // 2
    tab_b = finish(
        jnp.concatenate([cb, cb, cb, cb], axis=1),
        jnp.concatenate([-sb, z(hb), -sb, z(hb)], axis=1),
        jnp.concatenate([z(hb), sb, z(hb), sb], axis=1),
        jnp.ones((ctx_len, LANE), F32))
    return tab_a, tab_b


def _ab_weights(w_in, q_norm, w_q_up, kv_norm, w_kv_up, gq_norm, gk_norm):
    d = w_in.shape[0]
    o0 = Q_LORA
    o1 = o0 + KV_LORA
    o2 = o1 + ROPE_A
    o3 = o2 + H_B * HEAD_DIM
    o4 = o3 + KV_B * HEAD_DIM
    da = _deint(ROPE_A)
    db = _deint(HEAD_DIM)
    zeros = lambda *s: jnp.zeros(s, w_in.dtype)
    kr = jnp.concatenate([zeros(d, NOPE_A), w_in[:, o1:o2][:, da], zeros(d, LANE - NOPE_A - ROPE_A)], axis=1)
    gq = w_in[:, o2:o3].reshape(d, H_B, HEAD_DIM)[:, :, db]
    gq_blocks = []
    gqn_blocks = []
    gn = gq_norm[db]
    zn = jnp.zeros((HEAD_DIM,), gq_norm.dtype)
    for h in range(H_B):
        first = (h // G_B) == 0
        gq_blocks += [gq[:, h], zeros(d, HEAD_DIM)] if first else [zeros(d, HEAD_DIM), gq[:, h]]
        gqn_blocks += [gn, zn] if first else [zn, gn]
    gk = w_in[:, o3:o4].reshape(d, KV_B, HEAD_DIM)[:, :, db].reshape(d, KV_B * HEAD_DIM)
    win = jnp.concatenate([w_in[:, :o1], kr] + gq_blocks + [gk, w_in[:, o4:]], axis=1).astype(BF16)
    assert win.shape[1] == _C_END
    wq = w_q_up.reshape(Q_LORA, H_A, NOPE_A + ROPE_A)
    wq = jnp.concatenate([wq[:, :, :NOPE_A], wq[:, :, NOPE_A:][:, :, da],
                          jnp.zeros((Q_LORA, H_A, LANE - NOPE_A - ROPE_A), w_q_up.dtype)], axis=2)
    wkv = w_kv_up.reshape(KV_LORA, H_A, NOPE_A + V_A)
    wk = jnp.concatenate([wkv[:, :, :NOPE_A], jnp.zeros((KV_LORA, H_A, LANE - NOPE_A), w_kv_up.dtype)], axis=2)
    return {
        "win": win,
        "qn": q_norm.reshape(1, -1),
        "wq": wq.reshape(Q_LORA, H_A * LANE).astype(BF16),
        "kvn": kv_norm.reshape(1, -1),
        "wk": wk.reshape(KV_LORA, H_A * LANE).astype(BF16),
        "wv": wkv[:, :, NOPE_A:].reshape(KV_LORA, H_A * V_A).astype(BF16),
        "gqn": jnp.concatenate(gqn_blocks).reshape(1, -1),
        "gkn": jnp.tile(gk_norm[db], KV_B).reshape(1, -1),
    }


def _moe_layout(route, cnt, bsz, nt, total):
    counts = cnt[0, :N_EXPERTS].astype(jnp.int32)
    padded = ((counts + MOE_BM - 1) // MOE_BM) * MOE_BM
    ends = jnp.cumsum(padded)
    pad_start = ends - padded
    e = route[..., 0:2].astype(jnp.int32)
    rank = route[..., 2:4].astype(jnp.int32)
    dest = pad_start[e] + rank
    dest = dest.reshape(bsz, nt, TM, 2).transpose(0, 1, 3, 2).reshape(bsz * nt, 1, 2 * TM)
    nblk = total // MOE_BM
    blk_start = jnp.arange(nblk, dtype=jnp.int32) * MOE_BM
    blk_e = jnp.clip(jnp.searchsorted(ends, blk_start, side="right"), 0, N_EXPERTS - 1).astype(jnp.int32)
    nb_used = (ends[-1] // MOE_BM).astype(jnp.int32).reshape(1)
    return dest, blk_e, nb_used


def kernel(x, c, ctx, c_ctx, w_ada, b_ada, norm_mix, norm_ffn, norm_final, ab_w_in, ab_w_out, mla_q_norm,
           mla_w_q_up, mla_kv_norm, mla_w_kv_up, gqa_q_norm, gqa_k_norm, na_w_in, na_w_out, na_rpb, moe_w_rg,
           moe_b_rg, moe_w_re, moe_b_re, moe_w_gate, moe_w_up, moe_w_down):
    bsz, s_len, d = x.shape
    ctx_len = ctx.shape[1]
    depth = w_ada.shape[0]
    t = s_len + ctx_len
    assert s_len % TM == 0 and ctx_len == TM and s_len % GRID_W == 0 and bsz + 1 <= 8
    nt = t // TM
    n_lat_tiles = s_len // TM
    n_tok = bsz * t
    total = -(-(n_tok * 2 + N_EXPERTS * (MOE_BM - 1)) // MOE_BM) * MOE_BM

    cond = jnp.concatenate([c, c_ctx[None, :], jnp.zeros((8 - bsz - 1, d), c.dtype)], axis=0)
    mod = _ada_mod(cond.T, w_ada, b_ada, bsz + 1)
    mod3 = mod.reshape(depth * 8, 6, d)

    tabs = _rope_tables(s_len, ctx_len)
    tri = jnp.asarray(np.tril(np.ones((TM, TM), np.float32), -1), BF16)
    xall = jnp.concatenate([x, ctx], axis=1)

    for l in range(depth):
        last = l == depth - 1
        i = l // 2
        g_mix = norm_mix[l].reshape(1, d)
        g_ffn = norm_ffn[l].reshape(1, d)
        if l % 2 == 0:
            w = _ab_weights(ab_w_in[i], mla_q_norm[i], mla_w_q_up[i], mla_kv_norm[i], mla_w_kv_up[i],
                            gqa_q_norm[i], gqa_k_norm[i])
            qa, ka, va, qb, kb, vb = _ab_proj(xall, mod3, l, g_mix, w, tabs, n_lat_tiles, bsz)
            o1 = _flash(qa, ka, va, s_len, True)
            o2 = _flash(qb, kb, vb, s_len, False)
            o1_col = o2_col = 0
            wo = ab_w_out[i].astype(BF16)
        else:
            q, k, v = _na_proj(xall, mod3, l, g_mix, na_w_in[i].astype(BF16), n_lat_tiles, bsz)
            o1 = o2 = _na_attention(q, k, v, _na_bias_table(na_rpb[i]), s_len)
            o1_col, o2_col = 0, 1
            wo = na_w_out[i].astype(BF16)

        wr = jnp.concatenate([moe_w_re[l], moe_w_rg[l],
                              jnp.zeros((d, LANE - N_EXPERTS - N_GROUPS), F32)], axis=1).astype(F32)
        br = jnp.concatenate([moe_b_re[l], moe_b_rg[l],
                              jnp.zeros((LANE - N_EXPERTS - N_GROUPS,), F32)]).reshape(1, LANE).astype(F32)
        xall, route, cnt = _post(xall, o1, o2, o1_col, o2_col, wo, mod3, l, g_ffn, wr, br, tri,
                                 n_lat_tiles, bsz)
        dest, blk_e, nb_used = _moe_layout(route, cnt, bsz, nt, total)
        xs = _dispatch(dest, xall, mod3, l, g_ffn, total, n_lat_tiles, bsz)
        ys = _experts(blk_e, nb_used, xs, moe_w_gate, moe_w_up, moe_w_down, l)
        t_out = s_len if last else t
        xall = _combine(dest, xall, route, mod3, l, norm_final.reshape(1, d), ys, t_out, last,
                        n_lat_tiles, bsz)
    return xall
```

```python
import functools

import numpy as np
import jax
import jax.numpy as jnp
from jax import lax
from jax.experimental import pallas as pl
from jax.experimental.pallas import tpu as pltpu

F32 = jnp.float32
BF16 = jnp.bfloat16

GRID_W = 64
HEAD_DIM = 64
ROPE_THETA = 10000.0
RMS_EPS = 1e-6
H_A = 8
NOPE_A = 64
ROPE_A = 32
V_A = 64
Q_LORA = 384
KV_LORA = 256
H_B = 8
KV_B = 2
G_B = H_B // KV_B
H_C = 16
WIN_R = 8
WIN_C = 16
N_GROUPS = 4
EXPERTS_PER_GROUP = 8
N_EXPERTS = N_GROUPS * EXPERTS_PER_GROUP
D_EXPERT = 512

LANE = 128
TM = 256
MOE_BM = 256
NA_ROWS = TM // GRID_W
NEG = -1e30
LOG2E = 1.4426950408889634
FLASH_TK = 1024
VMEM_LIMIT = 48 * 1024 * 1024

_C_CQ = 0
_C_CKV = _C_CQ + Q_LORA
_C_KR = _C_CKV + KV_LORA
_C_GQ = _C_KR + LANE
_C_GK = _C_GQ + H_B * LANE
_C_END = _C_GK + KV_B * HEAD_DIM

VT_ROWS = LANE + 16


def _cparams(n_axes, vmem=VMEM_LIMIT):
    return pltpu.CompilerParams(dimension_semantics=("arbitrary",) * n_axes, vmem_limit_bytes=vmem)


def _rms(x):
    return x * lax.rsqrt(jnp.mean(x * x, axis=-1, keepdims=True) + RMS_EPS)


def _modulate(x, g, shift, scale):
    return (_rms(x) * g) * (1.0 + scale) + shift


def _silu(x):
    return x / (1.0 + jnp.exp(-x))


def _dot(a, b):
    return jnp.dot(a, b, preferred_element_type=F32)


def _dot_nt(a, b):
    return lax.dot_general(a, b, (((1,), (1,)), ((), ())), preferred_element_type=F32)


def _ada_kernel(c_ref, w_ref, b_ref, o_ref, *, n_vec):
    s = _silu(c_ref[...])
    w = w_ref[0]
    rows = []
    for r in range(8):
        if r < n_vec:
            rows.append(jnp.sum(w * s[:, r:r + 1], axis=0, keepdims=True) + b_ref[0])
        else:
            rows.append(jnp.zeros_like(b_ref[0]))
    o_ref[0] = jnp.concatenate(rows, axis=0)


def _ada_mod(cond_t, w_ada, b_ada, n_vec):
    depth, d, n6 = w_ada.shape
    tn = 512
    return pl.pallas_call(
        functools.partial(_ada_kernel, n_vec=n_vec),
        out_shape=jax.ShapeDtypeStruct((depth, 8, n6), F32),
        grid=(depth, n6 // tn),
        in_specs=[
            pl.BlockSpec((d, 8), lambda l, j: (0, 0)),
            pl.BlockSpec((1, d, tn), lambda l, j: (l, 0, j)),
            pl.BlockSpec((1, 1, tn), lambda l, j: (l, 0, j)),
        ],
        out_specs=pl.BlockSpec((1, 8, tn), lambda l, j: (l, 0, j)),
        compiler_params=_cparams(2),
    )(cond_t, w_ada, b_ada.reshape(depth, 1, n6))


def _rope(x, tab, sh):
    return (x * tab[:, 0:LANE]
            + pltpu.roll(x, LANE - sh, 1) * tab[:, LANE:2 * LANE]
            + pltpu.roll(x, sh, 1) * tab[:, 2 * LANE:3 * LANE])


def _with_ones_rows(vt):
    row = lax.broadcasted_iota(jnp.int32, vt.shape, 0)
    return jnp.where(row % VT_ROWS >= LANE, 1.0, vt)


def _ab_proj_kernel(x_ref, mod_ref, g_ref, win_ref, qn_ref, wq_ref, kvn_ref, wk_ref, wvt_ref, wgvt_ref,
                    gqn_ref, gkn_ref, ta_ref, tb_ref,
                    qa_ref, ka_ref, vat_ref, qb_ref, kb_ref, vbt_ref):
    mod = mod_ref[0]
    h = _modulate(x_ref[0], g_ref[...], mod[0:1], mod[1:2]).astype(BF16)
    p = _dot(h, win_ref[...])
    ta = ta_ref[...]
    tb = tb_ref[...]
    scale_a = (NOPE_A + ROPE_A) ** -0.5 * LOG2E
    scale_b = HEAD_DIM ** -0.5 * LOG2E

    cqn = (_rms(p[:, _C_CQ:_C_CKV]) * qn_ref[...]).astype(BF16)
    qa = _dot(cqn, wq_ref[...])
    ckvn = (_rms(p[:, _C_CKV:_C_KR]) * kvn_ref[...]).astype(BF16)
    ka = _dot(ckvn, wk_ref[...])
    vat_ref[0, 0] = _with_ones_rows(_dot_nt(wvt_ref[...], ckvn)).astype(BF16)
    kr = _rope(p[:, _C_KR:_C_GQ], ta, ROPE_A // 2)
    for hh in range(H_A):
        sl = slice(hh * LANE, (hh + 1) * LANE)
        qa_ref[0, :, sl] = (_rope(qa[:, sl], ta, ROPE_A // 2) * scale_a).astype(BF16)
        ka_ref[0, :, sl] = (ka[:, sl] + kr).astype(BF16)

    for hh in range(H_B):
        sl = slice(hh * LANE, (hh + 1) * LANE)
        xh = p[:, _C_GQ + hh * LANE:_C_GQ + (hh + 1) * LANE]
        r = lax.rsqrt(jnp.sum(xh * xh, axis=-1, keepdims=True) * (1.0 / HEAD_DIM) + RMS_EPS)
        qb_ref[0, :, sl] = (_rope(xh * r * gqn_ref[:, sl], tb, HEAD_DIM // 2) * scale_b).astype(BF16)
    gk = p[:, _C_GK:_C_END]
    lo = lax.broadcasted_iota(jnp.int32, gk.shape, 1) < HEAD_DIM
    sq = gk * gk
    s0 = jnp.sum(jnp.where(lo, sq, 0.0), axis=-1, keepdims=True)
    s1 = jnp.sum(jnp.where(lo, 0.0, sq), axis=-1, keepdims=True)
    r = jnp.where(lo, lax.rsqrt(s0 * (1.0 / HEAD_DIM) + RMS_EPS), lax.rsqrt(s1 * (1.0 / HEAD_DIM) + RMS_EPS))
    kb_ref[0] = _rope(gk * r * gkn_ref[...], tb, HEAD_DIM // 2).astype(BF16)
    vbt_ref[0, 0] = _with_ones_rows(_dot_nt(wgvt_ref[...], h)).astype(BF16)


def _ab_proj(xall, mod3, layer, g, w, tabs, n_lat_tiles, bsz):
    _, t, d = xall.shape
    nt = t // TM
    const = lambda b, i: (0, 0)
    tok = lambda b, i: (b, i, 0)

    def mod_map(b, i):
        return (layer * 8 + jnp.where(i >= n_lat_tiles, bsz, b), 0, 0)

    def full(a):
        return pl.BlockSpec(a.shape, const)

    def tok_out(n):
        return jax.ShapeDtypeStruct((bsz, t, n), BF16), pl.BlockSpec((1, TM, n), tok)

    def vt_out(rows):
        return (jax.ShapeDtypeStruct((bsz, nt, rows, TM), BF16),
                pl.BlockSpec((1, 1, rows, TM), lambda b, i: (b, i, 0, 0)))

    outs = [tok_out(H_A * LANE), tok_out(H_A * LANE), vt_out(H_A // 2 * VT_ROWS),
            tok_out(H_B * LANE), tok_out(KV_B * HEAD_DIM), vt_out(VT_ROWS)]
    return pl.pallas_call(
        _ab_proj_kernel,
        out_shape=[o[0] for o in outs],
        grid=(bsz, nt),
        in_specs=[
            pl.BlockSpec((1, TM, d), tok),
            pl.BlockSpec((1, 6, d), mod_map),
            full(g), full(w["win"]), full(w["qn"]), full(w["wq"]), full(w["kvn"]), full(w["wk"]),
            full(w["wvt"]), full(w["wgvt"]), full(w["gqn"]), full(w["gkn"]),
            pl.BlockSpec((TM, 3 * LANE), lambda b, i: (i, 0)),
            pl.BlockSpec((TM, 3 * LANE), lambda b, i: (i, 0)),
        ],
        out_specs=[o[1] for o in outs],
        compiler_params=_cparams(2),
    )(xall, mod3, g, w["win"], w["qn"], w["wq"], w["kvn"], w["wk"], w["wvt"], w["wgvt"], w["gqn"], w["gkn"],
      tabs[0], tabs[1])


def _flash_kernel(q_ref, k_ref, vt_ref, o_ref, st_ref, *, n_lat_tiles, s_len, tk, k_per_head):
    is_ctx = pl.program_id(2) >= n_lat_tiles
    qs = [q_ref[0, :, a * LANE:(a + 1) * LANE] for a in range(2)]
    tiles_per_chunk = tk // TM
    n_chunks = s_len // tk

    def scores(start, size, a):
        kc0 = a * LANE if k_per_head else 0
        return _dot_nt(k_ref[0, pl.ds(start, size), kc0:kc0 + LANE], qs[a])

    def update(st, vt, state):
        m, acc = state
        m_new = jnp.maximum(m, jnp.max(st, axis=0, keepdims=True))
        alpha = jnp.exp2(m - m_new)
        pexp = jnp.exp2(st - m_new).astype(BF16)
        return m_new, alpha * acc + _dot(vt, pexp)

    def qk_chunk(j, buf):
        for a in range(2):
            st_ref[buf, a] = scores(pl.multiple_of(j * tk, tk), tk, a)

    def pv_chunk(j, buf, carry):
        vt = jnp.concatenate([vt_ref[0, j * tiles_per_chunk + u] for u in range(tiles_per_chunk)], axis=1)
        return tuple(update(st_ref[buf, a], vt, carry[a]) for a in range(2))

    init = (jnp.full((1, TM), NEG, F32), jnp.zeros((VT_ROWS, TM), F32))
    vt_ctx = vt_ref[0, n_lat_tiles]
    ctx_only = tuple(update(scores(s_len, TM, a), vt_ctx, init) for a in range(2))

    qk_chunk(0, 0)

    def pair(jj, carry):
        c0 = 2 * jj
        qk_chunk(c0 + 1, 1)
        carry = pv_chunk(c0, 0, carry)
        qk_chunk(jnp.minimum(c0 + 2, n_chunks - 1), 0)
        return pv_chunk(c0 + 1, 1, carry)

    full = lax.fori_loop(0, n_chunks // 2, pair, ctx_only)
    if n_chunks % 2:
        full = pv_chunk(n_chunks - 1, 0, full)
    outs = []
    for (_, ac), (_, af) in zip(ctx_only, full):
        acc = jnp.where(is_ctx, ac, af)
        outs.append(acc[0:LANE] / acc[LANE:LANE + 1])
    hd = HEAD_DIM
    if k_per_head:
        ot = jnp.concatenate([outs[0][0:hd], outs[1][hd:2 * hd]], axis=0)
    else:
        kv_lo = (pl.program_id(1) // (G_B // 2)) == 0
        ot = jnp.concatenate([jnp.where(kv_lo, o[0:hd], o[hd:2 * hd]) for o in outs], axis=0)
    o_ref[0] = ot.T.astype(BF16)


def _flash(q, k, vt, s_len, k_per_head):
    bsz, t, qw = q.shape
    n_pairs = qw // (2 * LANE)
    nt = t // TM
    tk = min(FLASH_TK, max(TM, s_len // 4))
    assert s_len % tk == 0 and tk % TM == 0
    kw = 2 * LANE if k_per_head else LANE
    return pl.pallas_call(
        functools.partial(_flash_kernel, n_lat_tiles=s_len // TM, s_len=s_len, tk=tk, k_per_head=k_per_head),
        out_shape=jax.ShapeDtypeStruct((bsz, t, n_pairs * LANE), BF16),
        grid=(bsz, n_pairs, nt),
        in_specs=[
            pl.BlockSpec((1, TM, 2 * LANE), lambda b, p, i: (b, i, p)),
            pl.BlockSpec((1, t, kw), (lambda b, p, i: (b, 0, p)) if k_per_head else (lambda b, p, i: (b, 0, 0))),
            pl.BlockSpec((1, nt, VT_ROWS, TM),
                         (lambda b, p, i: (b, 0, p, 0)) if k_per_head else (lambda b, p, i: (b, 0, 0, 0))),
        ],
        out_specs=pl.BlockSpec((1, TM, LANE), lambda b, p, i: (b, i, p)),
        scratch_shapes=[pltpu.VMEM((2, 2, tk, TM), F32)],
        compiler_params=_cparams(3),
    )(q, k, vt)


def _na_proj_kernel(x_ref, mod_ref, g_ref, w_ref, q_ref, k_ref, v_ref):
    mod = mod_ref[0]
    h = _modulate(x_ref[0], g_ref[...], mod[0:1], mod[1:2]).astype(BF16)
    p = _dot(h, w_ref[...])
    cw = H_C * HEAD_DIM
    q_ref[0] = (p[:, 0:cw] * (HEAD_DIM ** -0.5)).astype(BF16)
    k_ref[0] = p[:, cw:2 * cw].astype(BF16)
    v_ref[0] = p[:, 2 * cw:3 * cw].astype(BF16)


def _na_proj(xall, mod3, layer, g, w, n_lat_tiles, bsz):
    _, t, d = xall.shape
    cw = H_C * HEAD_DIM
    tok = lambda b, i: (b, i, 0)
    return pl.pallas_call(
        _na_proj_kernel,
        out_shape=[jax.ShapeDtypeStruct((bsz, t, cw), BF16)] * 3,
        grid=(bsz, t // TM),
        in_specs=[
            pl.BlockSpec((1, TM, d), tok),
            pl.BlockSpec((1, 6, d), lambda b, i: (layer * 8 + jnp.where(i >= n_lat_tiles, bsz, b), 0, 0)),
            pl.BlockSpec(g.shape, lambda b, i: (0, 0)),
            pl.BlockSpec(w.shape, lambda b, i: (0, 0)),
        ],
        out_specs=[pl.BlockSpec((1, TM, cw), tok)] * 3,
        compiler_params=_cparams(2),
    )(xall, mod3, g, w)


def _na_kernel(q_ref, k_ref, v_ref, bias_ref, o_ref, *, n_lat_tiles, s_len, ctx_len, rows_n):
    i = pl.program_id(2)
    nwin = WIN_R * GRID_W
    kc = k_ref[0, s_len:s_len + ctx_len, :]
    vc = v_ref[0, s_len:s_len + ctx_len, :]

    @pl.when(i < n_lat_tiles)
    def _():
        lo = lax.broadcasted_iota(jnp.int32, (GRID_W, LANE), 1) < HEAD_DIM
        for rr in range(NA_ROWS):
            r = i * NA_ROWS + rr
            rs = jnp.clip(r - WIN_R // 2, 0, rows_n - WIN_R)
            var = r - rs
            start = pl.multiple_of(rs * GRID_W, GRID_W)
            kwin = k_ref[0, pl.ds(start, nwin), :]
            vwin = v_ref[0, pl.ds(start, nwin), :]
            q2 = q_ref[0, rr * GRID_W:(rr + 1) * GRID_W, :]
            outs = []
            for a in range(2):
                qa = jnp.where(lo if a == 0 else jnp.logical_not(lo), q2, jnp.zeros_like(q2))
                sn = _dot_nt(qa, kwin) + bias_ref[var, a]
                sc = _dot_nt(qa, kc)
                m = jnp.maximum(jnp.max(sn, axis=-1, keepdims=True), jnp.max(sc, axis=-1, keepdims=True))
                pn = jnp.exp(sn - m)
                pc = jnp.exp(sc - m)
                l = jnp.sum(pn, axis=-1, keepdims=True) + jnp.sum(pc, axis=-1, keepdims=True)
                outs.append((_dot(pn.astype(BF16), vwin) + _dot(pc.astype(BF16), vc)) / l)
            o_ref[0, rr * GRID_W:(rr + 1) * GRID_W, :] = jnp.where(lo, outs[0], outs[1]).astype(BF16)

    @pl.when(i >= n_lat_tiles)
    def _():
        lo = lax.broadcasted_iota(jnp.int32, (TM, LANE), 1) < HEAD_DIM
        q2 = q_ref[0]
        outs = []
        for a in range(2):
            qa = jnp.where(lo if a == 0 else jnp.logical_not(lo), q2, jnp.zeros_like(q2))
            sc = _dot_nt(qa, kc)
            m = jnp.max(sc, axis=-1, keepdims=True)
            pc = jnp.exp(sc - m)
            l = jnp.sum(pc, axis=-1, keepdims=True)
            outs.append(_dot(pc.astype(BF16), vc) / l)
        o_ref[0] = jnp.where(lo, outs[0], outs[1]).astype(BF16)


def _na_attention(q, k, v, bias, s_len):
    bsz, t, cw = q.shape
    n_pairs = cw // LANE
    rows_n = s_len // GRID_W
    assert rows_n >= WIN_R and GRID_W >= WIN_C
    tile = lambda b, p, i: (b, i, p)
    res = lambda b, p, i: (b, 0, p)
    return pl.pallas_call(
        functools.partial(_na_kernel, n_lat_tiles=s_len // TM, s_len=s_len, ctx_len=t - s_len, rows_n=rows_n),
        out_shape=jax.ShapeDtypeStruct((bsz, t, cw), BF16),
        grid=(bsz, n_pairs, t // TM),
        in_specs=[
            pl.BlockSpec((1, TM, LANE), tile),
            pl.BlockSpec((1, t, LANE), res),
            pl.BlockSpec((1, t, LANE), res),
            pl.BlockSpec((WIN_R, 2, GRID_W, WIN_R * GRID_W), lambda b, p, i: (0, p, 0, 0)),
        ],
        out_specs=pl.BlockSpec((1, TM, LANE), tile),
        compiler_params=_cparams(3),
    )(q, k, v, bias)


def _na_bias_table(rpb):
    rpb = rpb.astype(F32)
    rows = jnp.stack([rpb[:, WIN_R - 1 - v:2 * WIN_R - 1 - v, :] for v in range(WIN_R)], axis=0)
    rp = jnp.pad(rows, ((0, 0), (0, 0), (0, 0), (GRID_W, GRID_W)))
    off = GRID_W + WIN_C - 1
    vals = jnp.stack([rp[..., off - c:off - c + GRID_W] for c in range(GRID_W)], axis=2)
    col = np.arange(GRID_W)[:, None, None]
    kc = np.arange(GRID_W)[None, None, :]
    cs = np.clip(col - WIN_C // 2, 0, GRID_W - WIN_C)
    valid = np.broadcast_to((kc >= cs) & (kc < cs + WIN_C), (GRID_W, WIN_R, GRID_W))
    bias = jnp.where(valid[None, None], vals, NEG)
    return bias.reshape(WIN_R, rpb.shape[0], GRID_W, WIN_R * GRID_W)


def _post_kernel(x_ref, o1_ref, o2_ref, wo_ref, mod_ref, g_ref, wr_ref, br_ref, tri_ref,
                 xo_ref, route_ref, cnt_ref, base_ref):
    @pl.when((pl.program_id(0) == 0) & (pl.program_id(1) == 0))
    def _():
        base_ref[...] = jnp.zeros_like(base_ref)

    mod = mod_ref[0]
    half = wo_ref.shape[0] // 2
    o = _dot(o1_ref[0], wo_ref[0:half, :]) + _dot(o2_ref[0], wo_ref[half:2 * half, :])
    xn = x_ref[0] + mod[2:3] * o
    xo_ref[0] = xn

    h2 = _modulate(xn, g_ref[...], mod[3:4], mod[4:5])
    logits = jnp.dot(h2, wr_ref[...], preferred_element_type=F32, precision=lax.Precision.HIGHEST) + br_ref[...]
    lane = lax.broadcasted_iota(jnp.int32, logits.shape, 1)
    isg = (lane >= N_EXPERTS) & (lane < N_EXPERTS + N_GROUPS)
    lg = jnp.where(isg, logits, NEG)
    eg = jnp.where(isg, jnp.exp(lg - jnp.max(lg, axis=-1, keepdims=True)), 0.0)
    pg = eg / jnp.sum(eg, axis=-1, keepdims=True)
    p_gsel = jnp.max(pg, axis=-1, keepdims=True)
    gsel = jnp.min(jnp.where(isg & (pg == p_gsel), lane - N_EXPERTS, N_GROUPS), axis=-1, keepdims=True)
    sel = (lane < N_EXPERTS) & ((lane // EXPERTS_PER_GROUP) == gsel)
    le = jnp.where(sel, logits, NEG)
    ee = jnp.where(sel, jnp.exp(le - jnp.max(le, axis=-1, keepdims=True)), 0.0)
    pe = ee / jnp.sum(ee, axis=-1, keepdims=True)
    pm = jnp.where(sel, pe, -1.0)
    p1 = jnp.max(pm, axis=-1, keepdims=True)
    i1 = jnp.min(jnp.where(pm == p1, lane, LANE), axis=-1, keepdims=True)
    pm2 = jnp.where(lane == i1, -2.0, pm)
    p2 = jnp.max(pm2, axis=-1, keepdims=True)
    i2 = jnp.min(jnp.where(pm2 == p2, lane, LANE), axis=-1, keepdims=True)
    den = p1 + p2
    w1 = p_gsel * p1 / den
    w2 = p_gsel * p2 / den

    half_l = LANE // 2
    oh1 = lane == i1
    oh2 = lane == i2 + half_l
    oh = jnp.where(oh1 | oh2, 1.0, 0.0)
    oh_sw = jnp.where((lane == i2) | (lane == i1 + half_l), 1.0, 0.0)
    prefix = _dot(tri_ref[...], oh.astype(BF16))
    cnt = jnp.sum(oh, axis=0, keepdims=True)
    cnt_sw = jnp.sum(oh_sw, axis=0, keepdims=True)
    base = base_ref[...]
    lane1 = lax.broadcasted_iota(jnp.int32, base.shape, 1)
    tot = prefix + base + jnp.where(lane1 >= half_l, cnt_sw, 0.0)
    r1 = jnp.sum(jnp.where(oh1, tot, 0.0), axis=-1, keepdims=True)
    r2 = jnp.sum(jnp.where(oh2, tot, 0.0), axis=-1, keepdims=True)
    new_base = base + cnt + cnt_sw
    base_ref[...] = new_base
    cnt_ref[...] = new_base

    cols = [i1.astype(F32), i2.astype(F32), r1, r2, w1, w2]
    route = jnp.zeros(logits.shape, F32)
    for c, val in enumerate(cols):
        route = jnp.where(lane == c, val, route)
    route_ref[0] = route


def _post(xall, o1, o2, o1_col, o2_col, wo, mod3, layer, g, wr, br, tri, n_lat_tiles, bsz):
    _, t, d = xall.shape
    half = wo.shape[0] // 2
    tok = lambda b, i: (b, i, 0)
    const = lambda b, i: (0, 0)
    return pl.pallas_call(
        _post_kernel,
        out_shape=[jax.ShapeDtypeStruct((bsz, t, d), F32), jax.ShapeDtypeStruct((bsz, t, LANE), F32),
                   jax.ShapeDtypeStruct((1, LANE), F32)],
        grid=(bsz, t // TM),
        in_specs=[
            pl.BlockSpec((1, TM, d), tok),
            pl.BlockSpec((1, TM, half), lambda b, i: (b, i, o1_col)),
            pl.BlockSpec((1, TM, half), lambda b, i: (b, i, o2_col)),
            pl.BlockSpec(wo.shape, const),
            pl.BlockSpec((1, 6, d), lambda b, i: (layer * 8 + jnp.where(i >= n_lat_tiles, bsz, b), 0, 0)),
            pl.BlockSpec(g.shape, const),
            pl.BlockSpec(wr.shape, const),
            pl.BlockSpec(br.shape, const),
            pl.BlockSpec(tri.shape, const),
        ],
        out_specs=[pl.BlockSpec((1, TM, d), tok), pl.BlockSpec((1, TM, LANE), tok),
                   pl.BlockSpec((1, LANE), const)],
        scratch_shapes=[pltpu.VMEM((1, LANE), F32)],
        compiler_params=_cparams(2),
    )(xall, o1, o2, wo, mod3, g, wr, br, tri)


def _dispatch_kernel(dest_ref, x_ref, mod_ref, g_ref, xs_in_ref, xs_ref, hbuf, sem):
    del xs_in_ref
    mod = mod_ref[0]
    hbuf[...] = _modulate(x_ref[0], g_ref[...], mod[3:4], mod[4:5])

    def row_copy(j, dst):
        return pltpu.make_async_copy(hbuf.at[pl.ds(j, 1)], xs_ref.at[pl.ds(dst, 1)], sem)

    def issue(j, c):
        for k in range(2):
            row_copy(j, dest_ref[0, 0, k * TM + j]).start()
        return c

    lax.fori_loop(0, TM, issue, 0)

    def drain(j, c):
        for k in range(2):
            row_copy(j, dest_ref[0, 0, k * TM + j]).wait()
        return c

    lax.fori_loop(0, TM, drain, 0)


def _dispatch(dest, xall, mod3, layer, g, total, n_lat_tiles, bsz):
    _, t, d = xall.shape
    nt = t // TM
    xs0 = jnp.zeros((total, d), F32)
    return pl.pallas_call(
        _dispatch_kernel,
        out_shape=jax.ShapeDtypeStruct((total, d), F32),
        grid=(bsz, nt),
        in_specs=[
            pl.BlockSpec((1, 1, 2 * TM), lambda b, i: (b * nt + i, 0, 0), memory_space=pltpu.SMEM),
            pl.BlockSpec((1, TM, d), lambda b, i: (b, i, 0)),
            pl.BlockSpec((1, 6, d), lambda b, i: (layer * 8 + jnp.where(i >= n_lat_tiles, bsz, b), 0, 0)),
            pl.BlockSpec(g.shape, lambda b, i: (0, 0)),
            pl.BlockSpec(memory_space=pl.ANY),
        ],
        out_specs=pl.BlockSpec(memory_space=pl.ANY),
        scratch_shapes=[pltpu.VMEM((TM, d), F32), pltpu.SemaphoreType.DMA(())],
        input_output_aliases={4: 0},
        compiler_params=_cparams(2),
    )(dest, xall, mod3, g, xs0)


def _expert_kernel(be_ref, nb_ref, xs_ref, wg_ref, wu_ref, wd_ref, ys_ref, wgb, wub, wdb):
    i = pl.program_id(0)
    changed = (i == 0) | (be_ref[i] != be_ref[jnp.maximum(i - 1, 0)])

    @pl.when(changed)
    def _():
        wgb[...] = wg_ref[0, 0].astype(BF16)
        wub[...] = wu_ref[0, 0].astype(BF16)
        wdb[...] = wd_ref[0, 0].astype(BF16)

    @pl.when(i < nb_ref[0])
    def _():
        x = xs_ref[...].astype(BF16)
        a = _silu(_dot(x, wgb[...])) * _dot(x, wub[...])
        ys_ref[...] = _dot(a.astype(BF16), wdb[...])

    @pl.when(i >= nb_ref[0])
    def _():
        ys_ref[...] = jnp.zeros_like(ys_ref)


def _experts(blk_e, nb_used, xs, w_gate, w_up, w_down, layer):
    total, d = xs.shape
    de = w_gate.shape[-1]
    nblk = total // MOE_BM
    return pl.pallas_call(
        _expert_kernel,
        out_shape=jax.ShapeDtypeStruct((total, d), F32),
        grid_spec=pltpu.PrefetchScalarGridSpec(
            num_scalar_prefetch=2,
            grid=(nblk,),
            in_specs=[
                pl.BlockSpec((MOE_BM, d), lambda i, be, nb: (i, 0)),
                pl.BlockSpec((1, 1, d, de), lambda i, be, nb: (layer, be[i], 0, 0)),
                pl.BlockSpec((1, 1, d, de), lambda i, be, nb: (layer, be[i], 0, 0)),
                pl.BlockSpec((1, 1, de, d), lambda i, be, nb: (layer, be[i], 0, 0)),
            ],
            out_specs=pl.BlockSpec((MOE_BM, d), lambda i, be, nb: (i, 0)),
            scratch_shapes=[pltpu.VMEM((d, de), BF16), pltpu.VMEM((d, de), BF16), pltpu.VMEM((de, d), BF16)],
        ),
        compiler_params=_cparams(1),
    )(blk_e, nb_used, xs, w_gate, w_up, w_down)


def _combine_kernel(dest_ref, x_ref, route_ref, mod_ref, gf_ref, ys_ref, o_ref, ybuf, sem, *, final):
    def row_copy(j, k, src):
        return pltpu.make_async_copy(ys_ref.at[pl.ds(src, 1)], ybuf.at[pl.ds(k * TM + j, 1)], sem)

    def issue(j, c):
        for k in range(2):
            row_copy(j, k, dest_ref[0, 0, k * TM + j]).start()
        return c

    lax.fori_loop(0, TM, issue, 0)

    def drain(j, c):
        for k in range(2):
            row_copy(j, k, dest_ref[0, 0, k * TM + j]).wait()
        return c

    lax.fori_loop(0, TM, drain, 0)

    route = route_ref[0]
    y = ybuf[0:TM, :] * route[:, 4:5] + ybuf[TM:2 * TM, :] * route[:, 5:6]
    xn = x_ref[0] + mod_ref[0][5:6] * y
    if final:
        xn = _rms(xn) * gf_ref[...]
    o_ref[0] = xn


def _combine(dest, xall, route, mod3, layer, gf, ys, t_out, final, n_lat_tiles, bsz):
    _, t, d = xall.shape
    nt = t // TM
    tok = lambda b, i: (b, i, 0)
    return pl.pallas_call(
        functools.partial(_combine_kernel, final=final),
        out_shape=jax.ShapeDtypeStruct((bsz, t_out, d), F32),
        grid=(bsz, t_out // TM),
        in_specs=[
            pl.BlockSpec((1, 1, 2 * TM), lambda b, i: (b * nt + i, 0, 0), memory_space=pltpu.SMEM),
            pl.BlockSpec((1, TM, d), tok),
            pl.BlockSpec((1, TM, LANE), tok),
            pl.BlockSpec((1, 6, d), lambda b, i: (layer * 8 + jnp.where(i >= n_lat_tiles, bsz, b), 0, 0)),
            pl.BlockSpec(gf.shape, lambda b, i: (0, 0)),
            pl.BlockSpec(memory_space=pl.ANY),
        ],
        out_specs=pl.BlockSpec((1, TM, d), tok),
        scratch_shapes=[pltpu.VMEM((2 * TM, d), F32), pltpu.SemaphoreType.DMA(())],
        compiler_params=_cparams(2),
    )(dest, xall, route, mod3, gf, ys)


def _deint(n):
    return np.concatenate([np.arange(0, n, 2), np.arange(1, n, 2)])


def _rope_tables(s_len, ctx_len):
    tpos = jnp.arange(s_len, dtype=jnp.int32)
    r = (tpos // GRID_W).astype(F32)
    col = (tpos % GRID_W).astype(F32)

    def cos_sin(rot):
        n = rot // 4
        inv = ROPE_THETA ** (-jnp.arange(n, dtype=F32) / n)
        ang = jnp.concatenate([r[:, None] * inv, col[:, None] * inv], axis=-1)
        return jnp.cos(ang), jnp.sin(ang)

    def finish(cos_t, sinm_t, sinp_t, cos_ctx):
        zc = jnp.zeros((ctx_len, LANE), F32)
        return jnp.concatenate([
            jnp.concatenate([cos_t, sinm_t, sinp_t], axis=1),
            jnp.concatenate([cos_ctx, zc, zc], axis=1)], axis=0)

    ca, sa = cos_sin(ROPE_A)
    ha = ROPE_A // 2
    one = jnp.ones((s_len, NOPE_A), F32)
    z = lambda n: jnp.zeros((s_len, n), F32)
    tab_a = finish(
        jnp.concatenate([one, ca, ca, z(LANE - NOPE_A - ROPE_A)], axis=1),
        jnp.concatenate([z(NOPE_A), -sa, z(LANE - NOPE_A - ha)], axis=1),
        jnp.concatenate([z(NOPE_A + ha), sa, z(LANE - NOPE_A - ROPE_A)], axis=1),
        jnp.concatenate([jnp.ones((ctx_len, NOPE_A + ROPE_A), F32),
                         jnp.zeros((ctx_len, LANE - NOPE_A - ROPE_A), F32)], axis=1))
    cb, sb = cos_sin(HEAD_DIM)
    hb = HEAD_DIM // 2
    tab_b = finish(
        jnp.concatenate([cb, cb, cb, cb], axis=1),
        jnp.concatenate([-sb, z(hb), -sb, z(hb)], axis=1),
        jnp.concatenate([z(hb), sb, z(hb), sb], axis=1),
        jnp.ones((ctx_len, LANE), F32))
    return tab_a, tab_b


def _ab_weights(w_in, q_norm, w_q_up, kv_norm, w_kv_up, gq_norm, gk_norm):
    d = w_in.shape[0]
    o0 = Q_LORA
    o1 = o0 + KV_LORA
    o2 = o1 + ROPE_A
    o3 = o2 + H_B * HEAD_DIM
    o4 = o3 + KV_B * HEAD_DIM
    da = _deint(ROPE_A)
    db = _deint(HEAD_DIM)
    zeros = lambda *s: jnp.zeros(s, w_in.dtype)
    kr = jnp.concatenate([zeros(d, NOPE_A), w_in[:, o1:o2][:, da], zeros(d, LANE - NOPE_A - ROPE_A)], axis=1)
    gq = w_in[:, o2:o3].reshape(d, H_B, HEAD_DIM)[:, :, db]
    gq_blocks = []
    gqn_blocks = []
    gn = gq_norm[db]
    zn = jnp.zeros((HEAD_DIM,), gq_norm.dtype)
    for h in range(H_B):
        first = (h // G_B) == 0
        gq_blocks += [gq[:, h], zeros(d, HEAD_DIM)] if first else [zeros(d, HEAD_DIM), gq[:, h]]
        gqn_blocks += [gn, zn] if first else [zn, gn]
    gk = w_in[:, o3:o4].reshape(d, KV_B, HEAD_DIM)[:, :, db].reshape(d, KV_B * HEAD_DIM)
    win = jnp.concatenate([w_in[:, :o1], kr] + gq_blocks + [gk], axis=1).astype(BF16)
    assert win.shape[1] == _C_END
    pad_rows = VT_ROWS - LANE
    wgvt = jnp.concatenate([w_in[:, o4:].T, jnp.zeros((pad_rows, d), w_in.dtype)], axis=0).astype(BF16)
    wq = w_q_up.reshape(Q_LORA, H_A, NOPE_A + ROPE_A)
    wq = jnp.concatenate([wq[:, :, :NOPE_A], wq[:, :, NOPE_A:][:, :, da],
                          jnp.zeros((Q_LORA, H_A, LANE - NOPE_A - ROPE_A), w_q_up.dtype)], axis=2)
    wkv = w_kv_up.reshape(KV_LORA, H_A, NOPE_A + V_A)
    wk = jnp.concatenate([wkv[:, :, :NOPE_A], jnp.zeros((KV_LORA, H_A, LANE - NOPE_A), w_kv_up.dtype)], axis=2)
    wvt = wkv[:, :, NOPE_A:].reshape(KV_LORA, H_A // 2, 2 * V_A).transpose(1, 2, 0)
    wvt = jnp.concatenate([wvt, jnp.zeros((H_A // 2, pad_rows, KV_LORA), w_kv_up.dtype)], axis=1)
    return {
        "win": win,
        "qn": q_norm.reshape(1, -1),
        "wq": wq.reshape(Q_LORA, H_A * LANE).astype(BF16),
        "kvn": kv_norm.reshape(1, -1),
        "wk": wk.reshape(KV_LORA, H_A * LANE).astype(BF16),
        "wvt": wvt.reshape(H_A // 2 * VT_ROWS, KV_LORA).astype(BF16),
        "wgvt": wgvt,
        "gqn": jnp.concatenate(gqn_blocks).reshape(1, -1),
        "gkn": jnp.tile(gk_norm[db], KV_B).reshape(1, -1),
    }


def _moe_layout(route, cnt, bsz, nt, total):
    counts = cnt[0, :N_EXPERTS].astype(jnp.int32)
    padded = ((counts + MOE_BM - 1) // MOE_BM) * MOE_BM
    ends = jnp.cumsum(padded)
    pad_start = ends - padded
    e = route[..., 0:2].astype(jnp.int32)
    rank = route[..., 2:4].astype(jnp.int32)
    eid = jnp.arange(N_EXPERTS, dtype=jnp.int32)
    dest = jnp.sum(jnp.where(e[..., None] == eid, pad_start, 0), axis=-1) + rank
    dest = dest.reshape(bsz, nt, TM, 2).transpose(0, 1, 3, 2).reshape(bsz * nt, 1, 2 * TM)
    nblk = total // MOE_BM
    blk_start = jnp.arange(nblk, dtype=jnp.int32) * MOE_BM
    blk_e = jnp.sum((ends[None, :] <= blk_start[:, None]).astype(jnp.int32), axis=-1)
    blk_e = jnp.clip(blk_e, 0, N_EXPERTS - 1)
    nb_used = (ends[-1] // MOE_BM).astype(jnp.int32).reshape(1)
    return dest, blk_e, nb_used


def kernel(x, c, ctx, c_ctx, w_ada, b_ada, norm_mix, norm_ffn, norm_final, ab_w_in, ab_w_out, mla_q_norm,
           mla_w_q_up, mla_kv_norm, mla_w_kv_up, gqa_q_norm, gqa_k_norm, na_w_in, na_w_out, na_rpb, moe_w_rg,
           moe_b_rg, moe_w_re, moe_b_re, moe_w_gate, moe_w_up, moe_w_down):
    bsz, s_len, d = x.shape
    ctx_len = ctx.shape[1]
    depth = w_ada.shape[0]
    t = s_len + ctx_len
    assert s_len % TM == 0 and ctx_len == TM and s_len % GRID_W == 0 and bsz + 1 <= 8
    nt = t // TM
    n_lat_tiles = s_len // TM
    n_tok = bsz * t
    total = -(-(n_tok * 2 + N_EXPERTS * (MOE_BM - 1)) // MOE_BM) * MOE_BM

    cond = jnp.concatenate([c, c_ctx[None, :], jnp.zeros((8 - bsz - 1, d), c.dtype)], axis=0)
    mod = _ada_mod(cond.T, w_ada, b_ada, bsz + 1)
    mod3 = mod.reshape(depth * 8, 6, d)

    tabs = _rope_tables(s_len, ctx_len)
    tri = jnp.asarray(np.tril(np.ones((TM, TM), np.float32), -1), BF16)
    xall = jnp.concatenate([x, ctx], axis=1)

    for l in range(depth):
        last = l == depth - 1
        i = l // 2
        g_mix = norm_mix[l].reshape(1, d)
        g_ffn = norm_ffn[l].reshape(1, d)
        if l % 2 == 0:
            w = _ab_weights(ab_w_in[i], mla_q_norm[i], mla_w_q_up[i], mla_kv_norm[i], mla_w_kv_up[i],
                            gqa_q_norm[i], gqa_k_norm[i])
            qa, ka, va, qb, kb, vb = _ab_proj(xall, mod3, l, g_mix, w, tabs, n_lat_tiles, bsz)
            o1 = _flash(qa, ka, va, s_len, True)
            o2 = _flash(qb, kb, vb, s_len, False)
            o1_col = o2_col = 0
            wo = ab_w_out[i].astype(BF16)
        else:
            q, k, v = _na_proj(xall, mod3, l, g_mix, na_w_in[i].astype(BF16), n_lat_tiles, bsz)
            o1 = o2 = _na_attention(q, k, v, _na_bias_table(na_rpb[i]), s_len)
            o1_col, o2_col = 0, 1
            wo = na_w_out[i].astype(BF16)

        wr = jnp.concatenate([moe_w_re[l], moe_w_rg[l],
                              jnp.zeros((d, LANE - N_EXPERTS - N_GROUPS), F32)], axis=1).astype(F32)
        br = jnp.concatenate([moe_b_re[l], moe_b_rg[l],
                              jnp.zeros((LANE - N_EXPERTS - N_GROUPS,), F32)]).reshape(1, LANE).astype(F32)
        xall, route, cnt = _post(xall, o1, o2, o1_col, o2_col, wo, mod3, l, g_ffn, wr, br, tri,
                                 n_lat_tiles, bsz)
        dest, blk_e, nb_used = _moe_layout(route, cnt, bsz, nt, total)
        xs = _dispatch(dest, xall, mod3, l, g_ffn, total, n_lat_tiles, bsz)
        ys = _experts(blk_e, nb_used, xs, moe_w_gate, moe_w_up, moe_w_down, l)
        t_out = s_len if last else t
        xall = _combine(dest, xall, route, mod3, l, norm_final.reshape(1, d), ys, t_out, last,
                        n_lat_tiles, bsz)
    return xall
```

```python
import functools

import numpy as np
import jax
import jax.numpy as jnp
from jax import lax
from jax.experimental import pallas as pl
from jax.experimental.pallas import tpu as pltpu

F32 = jnp.float32
BF16 = jnp.bfloat16

GRID_W = 64
HEAD_DIM = 64
ROPE_THETA = 10000.0
RMS_EPS = 1e-6
H_A = 8
NOPE_A = 64
ROPE_A = 32
V_A = 64
Q_LORA = 384
KV_LORA = 256
H_B = 8
KV_B = 2
G_B = H_B // KV_B
H_C = 16
WIN_R = 8
WIN_C = 16
N_GROUPS = 4
EXPERTS_PER_GROUP = 8
N_EXPERTS = N_GROUPS * EXPERTS_PER_GROUP
D_EXPERT = 512

LANE = 128
TM = 256
MOE_BM = 256
NA_ROWS = TM // GRID_W
NEG = -1e30
LOG2E = 1.4426950408889634
FLASH_TK = 1024
DMA_UNROLL = 8
VMEM_LIMIT = 48 * 1024 * 1024

_C_CQ = 0
_C_CKV = _C_CQ + Q_LORA
_C_KR = _C_CKV + KV_LORA
_C_GQ = _C_KR + LANE
_C_GK = _C_GQ + H_B * LANE
_C_END = _C_GK + KV_B * HEAD_DIM

VT_ROWS = LANE + 16


def _cparams(n_axes, vmem=VMEM_LIMIT):
    return pltpu.CompilerParams(dimension_semantics=("arbitrary",) * n_axes, vmem_limit_bytes=vmem)


def _rms(x):
    return x * lax.rsqrt(jnp.mean(x * x, axis=-1, keepdims=True) + RMS_EPS)


def _modulate(x, g, shift, scale):
    return (_rms(x) * g) * (1.0 + scale) + shift


def _silu(x):
    return x / (1.0 + jnp.exp(-x))


def _dot(a, b):
    return jnp.dot(a, b, preferred_element_type=F32)


def _dot_nt(a, b):
    return lax.dot_general(a, b, (((1,), (1,)), ((), ())), preferred_element_type=F32)


def _ada_kernel(c_ref, w_ref, b_ref, o_ref, *, n_vec):
    s = _silu(c_ref[...])
    w = w_ref[0]
    rows = []
    for r in range(8):
        if r < n_vec:
            rows.append(jnp.sum(w * s[:, r:r + 1], axis=0, keepdims=True) + b_ref[0])
        else:
            rows.append(jnp.zeros_like(b_ref[0]))
    o_ref[0] = jnp.concatenate(rows, axis=0)


def _ada_mod(cond_t, w_ada, b_ada, n_vec):
    depth, d, n6 = w_ada.shape
    tn = 512
    return pl.pallas_call(
        functools.partial(_ada_kernel, n_vec=n_vec),
        out_shape=jax.ShapeDtypeStruct((depth, 8, n6), F32),
        grid=(depth, n6 // tn),
        in_specs=[
            pl.BlockSpec((d, 8), lambda l, j: (0, 0)),
            pl.BlockSpec((1, d, tn), lambda l, j: (l, 0, j)),
            pl.BlockSpec((1, 1, tn), lambda l, j: (l, 0, j)),
        ],
        out_specs=pl.BlockSpec((1, 8, tn), lambda l, j: (l, 0, j)),
        compiler_params=_cparams(2),
    )(cond_t, w_ada, b_ada.reshape(depth, 1, n6))


def _rope(x, tab, sh):
    return (x * tab[:, 0:LANE]
            + pltpu.roll(x, LANE - sh, 1) * tab[:, LANE:2 * LANE]
            + pltpu.roll(x, sh, 1) * tab[:, 2 * LANE:3 * LANE])


def _with_ones_rows(vt):
    row = lax.broadcasted_iota(jnp.int32, vt.shape, 0)
    return jnp.where(row % VT_ROWS >= LANE, 1.0, vt)


def _ab_proj_kernel(x_ref, mod_ref, g_ref, win_ref, qn_ref, wq_ref, kvn_ref, wk_ref, wvt_ref, wgvt_ref,
                    gqn_ref, gkn_ref, ta_ref, tb_ref,
                    qa_ref, ka_ref, vat_ref, qb_ref, kb_ref, vbt_ref):
    mod = mod_ref[0]
    h = _modulate(x_ref[0], g_ref[...], mod[0:1], mod[1:2]).astype(BF16)
    p = _dot(h, win_ref[...])
    ta = ta_ref[...]
    tb = tb_ref[...]
    scale_a = (NOPE_A + ROPE_A) ** -0.5 * LOG2E
    scale_b = HEAD_DIM ** -0.5 * LOG2E

    cqn = (_rms(p[:, _C_CQ:_C_CKV]) * qn_ref[...]).astype(BF16)
    qa = _dot(cqn, wq_ref[...])
    ckvn = (_rms(p[:, _C_CKV:_C_KR]) * kvn_ref[...]).astype(BF16)
    ka = _dot(ckvn, wk_ref[...])
    vat_ref[0, 0] = _with_ones_rows(_dot_nt(wvt_ref[...], ckvn)).astype(BF16)
    kr = _rope(p[:, _C_KR:_C_GQ], ta, ROPE_A // 2)
    for hh in range(H_A):
        sl = slice(hh * LANE, (hh + 1) * LANE)
        qa_ref[0, :, sl] = (_rope(qa[:, sl], ta, ROPE_A // 2) * scale_a).astype(BF16)
        ka_ref[0, :, sl] = (ka[:, sl] + kr).astype(BF16)

    for hh in range(H_B):
        sl = slice(hh * LANE, (hh + 1) * LANE)
        xh = p[:, _C_GQ + hh * LANE:_C_GQ + (hh + 1) * LANE]
        r = lax.rsqrt(jnp.sum(xh * xh, axis=-1, keepdims=True) * (1.0 / HEAD_DIM) + RMS_EPS)
        qb_ref[0, :, sl] = (_rope(xh * r * gqn_ref[:, sl], tb, HEAD_DIM // 2) * scale_b).astype(BF16)
    gk = p[:, _C_GK:_C_END]
    lo = lax.broadcasted_iota(jnp.int32, gk.shape, 1) < HEAD_DIM
    sq = gk * gk
    s0 = jnp.sum(jnp.where(lo, sq, 0.0), axis=-1, keepdims=True)
    s1 = jnp.sum(jnp.where(lo, 0.0, sq), axis=-1, keepdims=True)
    r = jnp.where(lo, lax.rsqrt(s0 * (1.0 / HEAD_DIM) + RMS_EPS), lax.rsqrt(s1 * (1.0 / HEAD_DIM) + RMS_EPS))
    kb_ref[0] = _rope(gk * r * gkn_ref[...], tb, HEAD_DIM // 2).astype(BF16)
    vbt_ref[0, 0] = _with_ones_rows(_dot_nt(wgvt_ref[...], h)).astype(BF16)


def _ab_proj(xall, mod3, layer, g, w, tabs, n_lat_tiles, bsz):
    _, t, d = xall.shape
    nt = t // TM
    const = lambda b, i: (0, 0)
    tok = lambda b, i: (b, i, 0)

    def mod_map(b, i):
        return (layer * 8 + jnp.where(i >= n_lat_tiles, bsz, b), 0, 0)

    def full(a):
        return pl.BlockSpec(a.shape, const)

    def tok_out(n):
        return jax.ShapeDtypeStruct((bsz, t, n), BF16), pl.BlockSpec((1, TM, n), tok)

    def vt_out(rows):
        return (jax.ShapeDtypeStruct((bsz, nt, rows, TM), BF16),
                pl.BlockSpec((1, 1, rows, TM), lambda b, i: (b, i, 0, 0)))

    outs = [tok_out(H_A * LANE), tok_out(H_A * LANE), vt_out(H_A // 2 * VT_ROWS),
            tok_out(H_B * LANE), tok_out(KV_B * HEAD_DIM), vt_out(VT_ROWS)]
    return pl.pallas_call(
        _ab_proj_kernel,
        out_shape=[o[0] for o in outs],
        grid=(bsz, nt),
        in_specs=[
            pl.BlockSpec((1, TM, d), tok),
            pl.BlockSpec((1, 6, d), mod_map),
            full(g), full(w["win"]), full(w["qn"]), full(w["wq"]), full(w["kvn"]), full(w["wk"]),
            full(w["wvt"]), full(w["wgvt"]), full(w["gqn"]), full(w["gkn"]),
            pl.BlockSpec((TM, 3 * LANE), lambda b, i: (i, 0)),
            pl.BlockSpec((TM, 3 * LANE), lambda b, i: (i, 0)),
        ],
        out_specs=[o[1] for o in outs],
        compiler_params=_cparams(2),
    )(xall, mod3, g, w["win"], w["qn"], w["wq"], w["kvn"], w["wk"], w["wvt"], w["wgvt"], w["gqn"], w["gkn"],
      tabs[0], tabs[1])


def _flash_kernel(q_ref, k_ref, vt_ref, o_ref, st_ref, *, n_lat_tiles, s_len, tk, k_per_head):
    is_ctx = pl.program_id(2) >= n_lat_tiles
    qs = [q_ref[0, :, a * LANE:(a + 1) * LANE] for a in range(2)]
    tiles_per_chunk = tk // TM
    n_chunks = s_len // tk

    def scores(start, size, a):
        kc0 = a * LANE if k_per_head else 0
        return _dot_nt(k_ref[0, pl.ds(start, size), kc0:kc0 + LANE], qs[a])

    def update(st, vt, state):
        m, acc = state
        m_new = jnp.maximum(m, jnp.max(st, axis=0, keepdims=True))
        alpha = jnp.exp2(m - m_new)
        pexp = jnp.exp2(st - m_new).astype(BF16)
        return m_new, alpha * acc + _dot(vt, pexp)

    def qk_chunk(j, buf):
        for a in range(2):
            st_ref[buf, a] = scores(pl.multiple_of(j * tk, tk), tk, a)

    def pv_chunk(j, buf, carry):
        vt = jnp.concatenate([vt_ref[0, j * tiles_per_chunk + u] for u in range(tiles_per_chunk)], axis=1)
        return tuple(update(st_ref[buf, a], vt, carry[a]) for a in range(2))

    init = (jnp.full((1, TM), NEG, F32), jnp.zeros((VT_ROWS, TM), F32))
    vt_ctx = vt_ref[0, n_lat_tiles]
    st_ctx = [scores(s_len, TM, a) for a in range(2)]
    qk_chunk(0, 0)
    ctx_only = tuple(update(st_ctx[a], vt_ctx, init) for a in range(2))

    def pair(jj, carry):
        c0 = 2 * jj
        qk_chunk(c0 + 1, 1)
        carry = pv_chunk(c0, 0, carry)
        qk_chunk(jnp.minimum(c0 + 2, n_chunks - 1), 0)
        return pv_chunk(c0 + 1, 1, carry)

    full = lax.fori_loop(0, n_chunks // 2, pair, ctx_only)
    if n_chunks % 2:
        full = pv_chunk(n_chunks - 1, 0, full)
    outs = []
    for (_, ac), (_, af) in zip(ctx_only, full):
        acc = jnp.where(is_ctx, ac, af)
        outs.append(acc[0:LANE] / acc[LANE:LANE + 1])
    hd = HEAD_DIM
    if k_per_head:
        ot = jnp.concatenate([outs[0][0:hd], outs[1][hd:2 * hd]], axis=0)
    else:
        kv_lo = (pl.program_id(1) // (G_B // 2)) == 0
        ot = jnp.concatenate([jnp.where(kv_lo, o[0:hd], o[hd:2 * hd]) for o in outs], axis=0)
    o_ref[0] = ot.T.astype(BF16)


def _flash(q, k, vt, s_len, k_per_head):
    bsz, t, qw = q.shape
    n_pairs = qw // (2 * LANE)
    nt = t // TM
    tk = min(FLASH_TK, max(TM, s_len // 4))
    assert s_len % tk == 0 and tk % TM == 0
    kw = 2 * LANE if k_per_head else LANE
    return pl.pallas_call(
        functools.partial(_flash_kernel, n_lat_tiles=s_len // TM, s_len=s_len, tk=tk, k_per_head=k_per_head),
        out_shape=jax.ShapeDtypeStruct((bsz, t, n_pairs * LANE), BF16),
        grid=(bsz, n_pairs, nt),
        in_specs=[
            pl.BlockSpec((1, TM, 2 * LANE), lambda b, p, i: (b, i, p)),
            pl.BlockSpec((1, t, kw), (lambda b, p, i: (b, 0, p)) if k_per_head else (lambda b, p, i: (b, 0, 0))),
            pl.BlockSpec((1, nt, VT_ROWS, TM),
                         (lambda b, p, i: (b, 0, p, 0)) if k_per_head else (lambda b, p, i: (b, 0, 0, 0))),
        ],
        out_specs=pl.BlockSpec((1, TM, LANE), lambda b, p, i: (b, i, p)),
        scratch_shapes=[pltpu.VMEM((2, 2, tk, TM), F32)],
        compiler_params=_cparams(3),
    )(q, k, vt)


def _na_proj_kernel(x_ref, mod_ref, g_ref, w_ref, q_ref, k_ref, v_ref):
    mod = mod_ref[0]
    h = _modulate(x_ref[0], g_ref[...], mod[0:1], mod[1:2]).astype(BF16)
    p = _dot(h, w_ref[...])
    cw = H_C * HEAD_DIM
    q_ref[0] = (p[:, 0:cw] * (HEAD_DIM ** -0.5 * LOG2E)).astype(BF16)
    k_ref[0] = p[:, cw:2 * cw].astype(BF16)
    v_ref[0] = p[:, 2 * cw:3 * cw].astype(BF16)


def _na_proj(xall, mod3, layer, g, w, n_lat_tiles, bsz):
    _, t, d = xall.shape
    cw = H_C * HEAD_DIM
    tok = lambda b, i: (b, i, 0)
    return pl.pallas_call(
        _na_proj_kernel,
        out_shape=[jax.ShapeDtypeStruct((bsz, t, cw), BF16)] * 3,
        grid=(bsz, t // TM),
        in_specs=[
            pl.BlockSpec((1, TM, d), tok),
            pl.BlockSpec((1, 6, d), lambda b, i: (layer * 8 + jnp.where(i >= n_lat_tiles, bsz, b), 0, 0)),
            pl.BlockSpec(g.shape, lambda b, i: (0, 0)),
            pl.BlockSpec(w.shape, lambda b, i: (0, 0)),
        ],
        out_specs=[pl.BlockSpec((1, TM, cw), tok)] * 3,
        compiler_params=_cparams(2),
    )(xall, mod3, g, w)


def _na_kernel(q_ref, k_ref, v_ref, bias_ref, o_ref, *, n_lat_tiles, s_len, ctx_len, rows_n):
    i = pl.program_id(2)
    nwin = WIN_R * GRID_W
    gw = GRID_W
    kc = k_ref[0, s_len:s_len + ctx_len, :]
    vc = v_ref[0, s_len:s_len + ctx_len, :]
    q2 = q_ref[0]
    lo = lax.broadcasted_iota(jnp.int32, (TM, LANE), 1) < HEAD_DIM
    zero = jnp.zeros_like(q2)
    qs = jnp.concatenate([jnp.where(lo, q2, zero), jnp.where(lo, zero, q2)], axis=0)
    sc_all = _dot_nt(qs, kc)

    def unstack(o):
        n = o.shape[0] // 2
        lo_n = lax.broadcasted_iota(jnp.int32, (n, LANE), 1) < HEAD_DIM
        return jnp.where(lo_n, o[0:n], o[n:2 * n])

    @pl.when(i < n_lat_tiles)
    def _():
        o_n, p_c, l_inv = [], [], []
        for rr in range(NA_ROWS):
            r = i * NA_ROWS + rr
            rs = jnp.clip(r - WIN_R // 2, 0, rows_n - WIN_R)
            var = r - rs
            start = pl.multiple_of(rs * gw, gw)
            kwin = k_ref[0, pl.ds(start, nwin), :]
            vwin = v_ref[0, pl.ds(start, nwin), :]
            rows = [slice(a * TM + rr * gw, a * TM + (rr + 1) * gw) for a in range(2)]
            q_r = jnp.concatenate([qs[rows[0]], qs[rows[1]]], axis=0)
            sn = _dot_nt(q_r, kwin) + jnp.concatenate([bias_ref[var, 0], bias_ref[var, 1]], axis=0)
            sc = jnp.concatenate([sc_all[rows[0]], sc_all[rows[1]]], axis=0)
            m = jnp.maximum(jnp.max(sn, axis=-1, keepdims=True), jnp.max(sc, axis=-1, keepdims=True))
            pn = jnp.exp2(sn - m)
            pc = jnp.exp2(sc - m)
            l_inv.append(1.0 / (jnp.sum(pn, axis=-1, keepdims=True) + jnp.sum(pc, axis=-1, keepdims=True)))
            o_n.append(_dot(pn.astype(BF16), vwin))
            p_c.append(pc.astype(BF16))
        o_c = _dot(jnp.concatenate(p_c, axis=0), vc)
        for rr in range(NA_ROWS):
            o = (o_n[rr] + o_c[rr * 2 * gw:(rr + 1) * 2 * gw]) * l_inv[rr]
            o_ref[0, rr * gw:(rr + 1) * gw, :] = unstack(o).astype(BF16)

    @pl.when(i >= n_lat_tiles)
    def _():
        m = jnp.max(sc_all, axis=-1, keepdims=True)
        pc = jnp.exp2(sc_all - m)
        l = jnp.sum(pc, axis=-1, keepdims=True)
        o_ref[0] = unstack(_dot(pc.astype(BF16), vc) / l).astype(BF16)


def _na_attention(q, k, v, bias, s_len):
    bsz, t, cw = q.shape
    n_pairs = cw // LANE
    rows_n = s_len // GRID_W
    assert rows_n >= WIN_R and GRID_W >= WIN_C
    tile = lambda b, p, i: (b, i, p)
    res = lambda b, p, i: (b, 0, p)
    return pl.pallas_call(
        functools.partial(_na_kernel, n_lat_tiles=s_len // TM, s_len=s_len, ctx_len=t - s_len, rows_n=rows_n),
        out_shape=jax.ShapeDtypeStruct((bsz, t, cw), BF16),
        grid=(bsz, n_pairs, t // TM),
        in_specs=[
            pl.BlockSpec((1, TM, LANE), tile),
            pl.BlockSpec((1, t, LANE), res),
            pl.BlockSpec((1, t, LANE), res),
            pl.BlockSpec((WIN_R, 2, GRID_W, WIN_R * GRID_W), lambda b, p, i: (0, p, 0, 0)),
        ],
        out_specs=pl.BlockSpec((1, TM, LANE), tile),
        compiler_params=_cparams(3),
    )(q, k, v, bias)


def _na_bias_table(rpb):
    rpb = rpb.astype(F32)
    rows = jnp.stack([rpb[:, WIN_R - 1 - v:2 * WIN_R - 1 - v, :] for v in range(WIN_R)], axis=0)
    rp = jnp.pad(rows, ((0, 0), (0, 0), (0, 0), (GRID_W, GRID_W)))
    off = GRID_W + WIN_C - 1
    vals = jnp.stack([rp[..., off - c:off - c + GRID_W] for c in range(GRID_W)], axis=2)
    col = np.arange(GRID_W)[:, None, None]
    kc = np.arange(GRID_W)[None, None, :]
    cs = np.clip(col - WIN_C // 2, 0, GRID_W - WIN_C)
    valid = np.broadcast_to((kc >= cs) & (kc < cs + WIN_C), (GRID_W, WIN_R, GRID_W))
    bias = jnp.where(valid[None, None], vals * LOG2E, NEG)
    return bias.reshape(WIN_R, rpb.shape[0], GRID_W, WIN_R * GRID_W)


def _post_kernel(x_ref, o1_ref, o2_ref, wo_ref, mod_ref, g_ref, wrh_ref, wrl_ref, br_ref, tri_ref,
                 xo_ref, route_ref, cnt_ref, base_ref):
    @pl.when((pl.program_id(0) == 0) & (pl.program_id(1) == 0))
    def _():
        base_ref[...] = jnp.zeros_like(base_ref)

    mod = mod_ref[0]
    half = wo_ref.shape[0] // 2
    o = _dot(o1_ref[0], wo_ref[0:half, :]) + _dot(o2_ref[0], wo_ref[half:2 * half, :])
    xn = x_ref[0] + mod[2:3] * o
    xo_ref[0] = xn

    h2 = _modulate(xn, g_ref[...], mod[3:4], mod[4:5])
    h_hi = h2.astype(BF16)
    h_lo = (h2 - h_hi.astype(F32)).astype(BF16)
    logits = (_dot(h_hi, wrh_ref[...]) + (_dot(h_lo, wrh_ref[...]) + _dot(h_hi, wrl_ref[...]))) + br_ref[...]
    lane = lax.broadcasted_iota(jnp.int32, logits.shape, 1)
    isg = (lane >= N_EXPERTS) & (lane < N_EXPERTS + N_GROUPS)
    lg = jnp.where(isg, logits, NEG)
    eg = jnp.where(isg, jnp.exp(lg - jnp.max(lg, axis=-1, keepdims=True)), 0.0)
    pg = eg / jnp.sum(eg, axis=-1, keepdims=True)
    p_gsel = jnp.max(pg, axis=-1, keepdims=True)
    gsel = jnp.min(jnp.where(isg & (pg == p_gsel), lane - N_EXPERTS, N_GROUPS), axis=-1, keepdims=True)
    sel = (lane < N_EXPERTS) & ((lane // EXPERTS_PER_GROUP) == gsel)
    le = jnp.where(sel, logits, NEG)
    ee = jnp.where(sel, jnp.exp(le - jnp.max(le, axis=-1, keepdims=True)), 0.0)
    pe = ee / jnp.sum(ee, axis=-1, keepdims=True)
    pm = jnp.where(sel, pe, -1.0)
    p1 = jnp.max(pm, axis=-1, keepdims=True)
    i1 = jnp.min(jnp.where(pm == p1, lane, LANE), axis=-1, keepdims=True)
    pm2 = jnp.where(lane == i1, -2.0, pm)
    p2 = jnp.max(pm2, axis=-1, keepdims=True)
    i2 = jnp.min(jnp.where(pm2 == p2, lane, LANE), axis=-1, keepdims=True)
    den = p1 + p2
    w1 = p_gsel * p1 / den
    w2 = p_gsel * p2 / den

    half_l = LANE // 2
    oh1 = lane == i1
    oh2 = lane == i2 + half_l
    oh = jnp.where(oh1 | oh2, 1.0, 0.0)
    oh_sw = jnp.where((lane == i2) | (lane == i1 + half_l), 1.0, 0.0)
    prefix = _dot(tri_ref[...], oh.astype(BF16))
    cnt = jnp.sum(oh, axis=0, keepdims=True)
    cnt_sw = jnp.sum(oh_sw, axis=0, keepdims=True)
    base = base_ref[...]
    lane1 = lax.broadcasted_iota(jnp.int32, base.shape, 1)
    tot = prefix + base + jnp.where(lane1 >= half_l, cnt_sw, 0.0)
    r1 = jnp.sum(jnp.where(oh1, tot, 0.0), axis=-1, keepdims=True)
    r2 = jnp.sum(jnp.where(oh2, tot, 0.0), axis=-1, keepdims=True)
    new_base = base + cnt + cnt_sw
    base_ref[...] = new_base
    cnt_ref[...] = new_base

    cols = [i1.astype(F32), i2.astype(F32), r1, r2, w1, w2]
    route = jnp.zeros(logits.shape, F32)
    for c, val in enumerate(cols):
        route = jnp.where(lane == c, val, route)
    route_ref[0] = route


def _post(xall, o1, o2, o1_col, o2_col, wo, mod3, layer, g, wr, br, tri, n_lat_tiles, bsz):
    _, t, d = xall.shape
    half = wo.shape[0] // 2
    tok = lambda b, i: (b, i, 0)
    const = lambda b, i: (0, 0)
    wr_hi = wr.astype(BF16)
    wr_lo = (wr - wr_hi.astype(F32)).astype(BF16)
    return pl.pallas_call(
        _post_kernel,
        out_shape=[jax.ShapeDtypeStruct((bsz, t, d), F32), jax.ShapeDtypeStruct((bsz, t, LANE), F32),
                   jax.ShapeDtypeStruct((1, LANE), F32)],
        grid=(bsz, t // TM),
        in_specs=[
            pl.BlockSpec((1, TM, d), tok),
            pl.BlockSpec((1, TM, half), lambda b, i: (b, i, o1_col)),
            pl.BlockSpec((1, TM, half), lambda b, i: (b, i, o2_col)),
            pl.BlockSpec(wo.shape, const),
            pl.BlockSpec((1, 6, d), lambda b, i: (layer * 8 + jnp.where(i >= n_lat_tiles, bsz, b), 0, 0)),
            pl.BlockSpec(g.shape, const),
            pl.BlockSpec(wr.shape, const),
            pl.BlockSpec(wr.shape, const),
            pl.BlockSpec(br.shape, const),
            pl.BlockSpec(tri.shape, const),
        ],
        out_specs=[pl.BlockSpec((1, TM, d), tok), pl.BlockSpec((1, TM, LANE), tok),
                   pl.BlockSpec((1, LANE), const)],
        scratch_shapes=[pltpu.VMEM((1, LANE), F32)],
        compiler_params=_cparams(2),
    )(xall, o1, o2, wo, mod3, g, wr_hi, wr_lo, br, tri)


def _dispatch_kernel(dest_ref, x_ref, mod_ref, g_ref, xs_in_ref, xs_ref, hbuf, sem):
    del xs_in_ref
    mod = mod_ref[0]
    h = _modulate(x_ref[0], g_ref[...], mod[3:4], mod[4:5])
    bits = lax.bitcast_convert_type(h.astype(BF16).astype(F32), jnp.uint32)
    half = bits.shape[1] // 2
    hbuf[...] = (bits[:, :half] >> 16) | bits[:, half:]

    def row_copy(j, dst):
        return pltpu.make_async_copy(hbuf.at[pl.ds(j, 1)], xs_ref.at[pl.ds(dst, 1)], sem)

    def issue(j, c):
        for k in range(2):
            row_copy(j, dest_ref[0, 0, k * TM + j]).start()
        return c

    lax.fori_loop(0, TM, issue, 0, unroll=DMA_UNROLL)

    def drain(j, c):
        for k in range(2):
            row_copy(j, dest_ref[0, 0, k * TM + j]).wait()
        return c

    lax.fori_loop(0, TM, drain, 0, unroll=DMA_UNROLL)


def _dispatch(dest, xall, mod3, layer, g, total, n_lat_tiles, bsz):
    _, t, d = xall.shape
    nt = t // TM
    xs0 = jnp.zeros((total, d // 2), jnp.uint32)
    return pl.pallas_call(
        _dispatch_kernel,
        out_shape=jax.ShapeDtypeStruct((total, d // 2), jnp.uint32),
        grid=(bsz, nt),
        in_specs=[
            pl.BlockSpec((1, 1, 2 * TM), lambda b, i: (b * nt + i, 0, 0), memory_space=pltpu.SMEM),
            pl.BlockSpec((1, TM, d), lambda b, i: (b, i, 0)),
            pl.BlockSpec((1, 6, d), lambda b, i: (layer * 8 + jnp.where(i >= n_lat_tiles, bsz, b), 0, 0)),
            pl.BlockSpec(g.shape, lambda b, i: (0, 0)),
            pl.BlockSpec(memory_space=pl.ANY),
        ],
        out_specs=pl.BlockSpec(memory_space=pl.ANY),
        scratch_shapes=[pltpu.VMEM((TM, d // 2), jnp.uint32), pltpu.SemaphoreType.DMA(())],
        input_output_aliases={4: 0},
        compiler_params=_cparams(2),
    )(dest, xall, mod3, g, xs0)


def _expert_kernel(be_ref, nb_ref, xs_ref, wg_ref, wu_ref, wd_ref, ys_ref, wgb, wub, wdb):
    i = pl.program_id(0)
    changed = (i == 0) | (be_ref[i] != be_ref[jnp.maximum(i - 1, 0)])

    @pl.when(changed)
    def _():
        wgb[...] = wg_ref[0, 0].astype(BF16)
        wub[...] = wu_ref[0, 0].astype(BF16)
        wdb[...] = wd_ref[0, 0].astype(BF16)

    @pl.when(i < nb_ref[0])
    def _():
        pk = xs_ref[...]
        x = jnp.concatenate([lax.bitcast_convert_type(pk << 16, F32),
                             lax.bitcast_convert_type(pk & jnp.uint32(0xFFFF0000), F32)], axis=1).astype(BF16)
        a = _silu(_dot(x, wgb[...])) * _dot(x, wub[...])
        ys_ref[...] = _dot(a.astype(BF16), wdb[...])

    @pl.when(i >= nb_ref[0])
    def _():
        ys_ref[...] = jnp.zeros_like(ys_ref)


def _experts(blk_e, nb_used, xs, w_gate, w_up, w_down, layer):
    total = xs.shape[0]
    d, de = w_gate.shape[-2:]
    nblk = total // MOE_BM
    return pl.pallas_call(
        _expert_kernel,
        out_shape=jax.ShapeDtypeStruct((total, d), F32),
        grid_spec=pltpu.PrefetchScalarGridSpec(
            num_scalar_prefetch=2,
            grid=(nblk,),
            in_specs=[
                pl.BlockSpec((MOE_BM, d // 2), lambda i, be, nb: (i, 0)),
                pl.BlockSpec((1, 1, d, de), lambda i, be, nb: (layer, be[i], 0, 0)),
                pl.BlockSpec((1, 1, d, de), lambda i, be, nb: (layer, be[i], 0, 0)),
                pl.BlockSpec((1, 1, de, d), lambda i, be, nb: (layer, be[i], 0, 0)),
            ],
            out_specs=pl.BlockSpec((MOE_BM, d), lambda i, be, nb: (i, 0)),
            scratch_shapes=[pltpu.VMEM((d, de), BF16), pltpu.VMEM((d, de), BF16), pltpu.VMEM((de, d), BF16)],
        ),
        compiler_params=_cparams(1),
    )(blk_e, nb_used, xs, w_gate, w_up, w_down)


def _combine_kernel(dest_ref, x_ref, route_ref, mod_ref, gf_ref, ys_ref, o_ref, ybuf, sem, *, final):
    def row_copy(j, k, src):
        return pltpu.make_async_copy(ys_ref.at[pl.ds(src, 1)], ybuf.at[pl.ds(k * TM + j, 1)], sem)

    def issue(j, c):
        for k in range(2):
            row_copy(j, k, dest_ref[0, 0, k * TM + j]).start()
        return c

    lax.fori_loop(0, TM, issue, 0, unroll=DMA_UNROLL)

    def drain(j, c):
        for k in range(2):
            row_copy(j, k, dest_ref[0, 0, k * TM + j]).wait()
        return c

    lax.fori_loop(0, TM, drain, 0, unroll=DMA_UNROLL)

    route = route_ref[0]
    y = ybuf[0:TM, :] * route[:, 4:5] + ybuf[TM:2 * TM, :] * route[:, 5:6]
    xn = x_ref[0] + mod_ref[0][5:6] * y
    if final:
        xn = _rms(xn) * gf_ref[...]
    o_ref[0] = xn


def _combine(dest, xall, route, mod3, layer, gf, ys, t_out, final, n_lat_tiles, bsz):
    _, t, d = xall.shape
    nt = t // TM
    tok = lambda b, i: (b, i, 0)
    return pl.pallas_call(
        functools.partial(_combine_kernel, final=final),
        out_shape=jax.ShapeDtypeStruct((bsz, t_out, d), F32),
        grid=(bsz, t_out // TM),
        in_specs=[
            pl.BlockSpec((1, 1, 2 * TM), lambda b, i: (b * nt + i, 0, 0), memory_space=pltpu.SMEM),
            pl.BlockSpec((1, TM, d), tok),
            pl.BlockSpec((1, TM, LANE), tok),
            pl.BlockSpec((1, 6, d), lambda b, i: (layer * 8 + jnp.where(i >= n_lat_tiles, bsz, b), 0, 0)),
            pl.BlockSpec(gf.shape, lambda b, i: (0, 0)),
            pl.BlockSpec(memory_space=pl.ANY),
        ],
        out_specs=pl.BlockSpec((1, TM, d), tok),
        scratch_shapes=[pltpu.VMEM((2 * TM, d), F32), pltpu.SemaphoreType.DMA(())],
        compiler_params=_cparams(2),
    )(dest, xall, route, mod3, gf, ys)


def _deint(n):
    return np.concatenate([np.arange(0, n, 2), np.arange(1, n, 2)])


def _rope_tables(s_len, ctx_len):
    tpos = jnp.arange(s_len, dtype=jnp.int32)
    r = (tpos // GRID_W).astype(F32)
    col = (tpos % GRID_W).astype(F32)

    def cos_sin(rot):
        n = rot // 4
        inv = ROPE_THETA ** (-jnp.arange(n, dtype=F32) / n)
        ang = jnp.concatenate([r[:, None] * inv, col[:, None] * inv], axis=-1)
        return jnp.cos(ang), jnp.sin(ang)

    def finish(cos_t, sinm_t, sinp_t, cos_ctx):
        zc = jnp.zeros((ctx_len, LANE), F32)
        return jnp.concatenate([
            jnp.concatenate([cos_t, sinm_t, sinp_t], axis=1),
            jnp.concatenate([cos_ctx, zc, zc], axis=1)], axis=0)

    ca, sa = cos_sin(ROPE_A)
    ha = ROPE_A // 2
    one = jnp.ones((s_len, NOPE_A), F32)
    z = lambda n: jnp.zeros((s_len, n), F32)
    tab_a = finish(
        jnp.concatenate([one, ca, ca, z(LANE - NOPE_A - ROPE_A)], axis=1),
        jnp.concatenate([z(NOPE_A), -sa, z(LANE - NOPE_A - ha)], axis=1),
        jnp.concatenate([z(NOPE_A + ha), sa, z(LANE - NOPE_A - ROPE_A)], axis=1),
        jnp.concatenate([jnp.ones((ctx_len, NOPE_A + ROPE_A), F32),
                         jnp.zeros((ctx_len, LANE - NOPE_A - ROPE_A), F32)], axis=1))
    cb, sb = cos_sin(HEAD_DIM)
    hb = HEAD_DIM // 2
    tab_b = finish(
        jnp.concatenate([cb, cb, cb, cb], axis=1),
        jnp.concatenate([-sb, z(hb), -sb, z(hb)], axis=1),
        jnp.concatenate([z(hb), sb, z(hb), sb], axis=1),
        jnp.ones((ctx_len, LANE), F32))
    return tab_a, tab_b


def _ab_weights(w_in, q_norm, w_q_up, kv_norm, w_kv_up, gq_norm, gk_norm):
    d = w_in.shape[0]
    o0 = Q_LORA
    o1 = o0 + KV_LORA
    o2 = o1 + ROPE_A
    o3 = o2 + H_B * HEAD_DIM
    o4 = o3 + KV_B * HEAD_DIM
    da = _deint(ROPE_A)
    db = _deint(HEAD_DIM)
    zeros = lambda *s: jnp.zeros(s, w_in.dtype)
    kr = jnp.concatenate([zeros(d, NOPE_A), w_in[:, o1:o2][:, da], zeros(d, LANE - NOPE_A - ROPE_A)], axis=1)
    gq = w_in[:, o2:o3].reshape(d, H_B, HEAD_DIM)[:, :, db]
    gq_blocks = []
    gqn_blocks = []
    gn = gq_norm[db]
    zn = jnp.zeros((HEAD_DIM,), gq_norm.dtype)
    for h in range(H_B):
        first = (h // G_B) == 0
        gq_blocks += [gq[:, h], zeros(d, HEAD_DIM)] if first else [zeros(d, HEAD_DIM), gq[:, h]]
        gqn_blocks += [gn, zn] if first else [zn, gn]
    gk = w_in[:, o3:o4].reshape(d, KV_B, HEAD_DIM)[:, :, db].reshape(d, KV_B * HEAD_DIM)
    win = jnp.concatenate([w_in[:, :o1], kr] + gq_blocks + [gk], axis=1).astype(BF16)
    assert win.shape[1] == _C_END
    pad_rows = VT_ROWS - LANE
    wgvt = jnp.concatenate([w_in[:, o4:].T, jnp.zeros((pad_rows, d), w_in.dtype)], axis=0).astype(BF16)
    wq = w_q_up.reshape(Q_LORA, H_A, NOPE_A + ROPE_A)
    wq = jnp.concatenate([wq[:, :, :NOPE_A], wq[:, :, NOPE_A:][:, :, da],
                          jnp.zeros((Q_LORA, H_A, LANE - NOPE_A - ROPE_A), w_q_up.dtype)], axis=2)
    wkv = w_kv_up.reshape(KV_LORA, H_A, NOPE_A + V_A)
    wk = jnp.concatenate([wkv[:, :, :NOPE_A], jnp.zeros((KV_LORA, H_A, LANE - NOPE_A), w_kv_up.dtype)], axis=2)
    wvt = wkv[:, :, NOPE_A:].reshape(KV_LORA, H_A // 2, 2 * V_A).transpose(1, 2, 0)
    wvt = jnp.concatenate([wvt, jnp.zeros((H_A // 2, pad_rows, KV_LORA), w_kv_up.dtype)], axis=1)
    return {
        "win": win,
        "qn": q_norm.reshape(1, -1),
        "wq": wq.reshape(Q_LORA, H_A * LANE).astype(BF16),
        "kvn": kv_norm.reshape(1, -1),
        "wk": wk.reshape(KV_LORA, H_A * LANE).astype(BF16),
        "wvt": wvt.reshape(H_A // 2 * VT_ROWS, KV_LORA).astype(BF16),
        "wgvt": wgvt,
        "gqn": jnp.concatenate(gqn_blocks).reshape(1, -1),
        "gkn": jnp.tile(gk_norm[db], KV_B).reshape(1, -1),
    }


def _moe_layout(route, cnt, bsz, nt, total):
    counts = cnt[0, :N_EXPERTS].astype(jnp.int32)
    padded = ((counts + MOE_BM - 1) // MOE_BM) * MOE_BM
    ends = jnp.cumsum(padded)
    pad_start = ends - padded
    e = route[..., 0:2].astype(jnp.int32)
    rank = route[..., 2:4].astype(jnp.int32)
    eid = jnp.arange(N_EXPERTS, dtype=jnp.int32)
    dest = jnp.sum(jnp.where(e[..., None] == eid, pad_start, 0), axis=-1) + rank
    dest = dest.reshape(bsz, nt, TM, 2).transpose(0, 1, 3, 2).reshape(bsz * nt, 1, 2 * TM)
    nblk = total // MOE_BM
    blk_start = jnp.arange(nblk, dtype=jnp.int32) * MOE_BM
    blk_e = jnp.sum((ends[None, :] <= blk_start[:, None]).astype(jnp.int32), axis=-1)
    blk_e = jnp.clip(blk_e, 0, N_EXPERTS - 1)
    nb_used = (ends[-1] // MOE_BM).astype(jnp.int32).reshape(1)
    return dest, blk_e, nb_used


def kernel(x, c, ctx, c_ctx, w_ada, b_ada, norm_mix, norm_ffn, norm_final, ab_w_in, ab_w_out, mla_q_norm,
           mla_w_q_up, mla_kv_norm, mla_w_kv_up, gqa_q_norm, gqa_k_norm, na_w_in, na_w_out, na_rpb, moe_w_rg,
           moe_b_rg, moe_w_re, moe_b_re, moe_w_gate, moe_w_up, moe_w_down):
    bsz, s_len, d = x.shape
    ctx_len = ctx.shape[1]
    depth = w_ada.shape[0]
    t = s_len + ctx_len
    assert s_len % TM == 0 and ctx_len == TM and s_len % GRID_W == 0 and bsz + 1 <= 8
    nt = t // TM
    n_lat_tiles = s_len // TM
    n_tok = bsz * t
    total = -(-(n_tok * 2 + N_EXPERTS * (MOE_BM - 1)) // MOE_BM) * MOE_BM

    cond = jnp.concatenate([c, c_ctx[None, :], jnp.zeros((8 - bsz - 1, d), c.dtype)], axis=0)
    mod = _ada_mod(cond.T, w_ada, b_ada, bsz + 1)
    mod3 = mod.reshape(depth * 8, 6, d)

    tabs = _rope_tables(s_len, ctx_len)
    tri = jnp.asarray(np.tril(np.ones((TM, TM), np.float32), -1), BF16)
    xall = jnp.concatenate([x, ctx], axis=1)

    for l in range(depth):
        last = l == depth - 1
        i = l // 2
        g_mix = norm_mix[l].reshape(1, d)
        g_ffn = norm_ffn[l].reshape(1, d)
        if l % 2 == 0:
            w = _ab_weights(ab_w_in[i], mla_q_norm[i], mla_w_q_up[i], mla_kv_norm[i], mla_w_kv_up[i],
                            gqa_q_norm[i], gqa_k_norm[i])
            qa, ka, va, qb, kb, vb = _ab_proj(xall, mod3, l, g_mix, w, tabs, n_lat_tiles, bsz)
            o1 = _flash(qa, ka, va, s_len, True)
            o2 = _flash(qb, kb, vb, s_len, False)
            o1_col = o2_col = 0
            wo = ab_w_out[i].astype(BF16)
        else:
            q, k, v = _na_proj(xall, mod3, l, g_mix, na_w_in[i].astype(BF16), n_lat_tiles, bsz)
            o1 = o2 = _na_attention(q, k, v, _na_bias_table(na_rpb[i]), s_len)
            o1_col, o2_col = 0, 1
            wo = na_w_out[i].astype(BF16)

        wr = jnp.concatenate([moe_w_re[l], moe_w_rg[l],
                              jnp.zeros((d, LANE - N_EXPERTS - N_GROUPS), F32)], axis=1).astype(F32)
        br = jnp.concatenate([moe_b_re[l], moe_b_rg[l],
                              jnp.zeros((LANE - N_EXPERTS - N_GROUPS,), F32)]).reshape(1, LANE).astype(F32)
        xall, route, cnt = _post(xall, o1, o2, o1_col, o2_col, wo, mod3, l, g_ffn, wr, br, tri,
                                 n_lat_tiles, bsz)
        dest, blk_e, nb_used = _moe_layout(route, cnt, bsz, nt, total)
        xs = _dispatch(dest, xall, mod3, l, g_ffn, total, n_lat_tiles, bsz)
        ys = _experts(blk_e, nb_used, xs, moe_w_gate, moe_w_up, moe_w_down, l)
        t_out = s_len if last else t
        xall = _combine(dest, xall, route, mod3, l, norm_final.reshape(1, d), ys, t_out, last,
                        n_lat_tiles, bsz)
    return xall
```

```python
import functools

import numpy as np
import jax
import jax.numpy as jnp
from jax import lax
from jax.experimental import pallas as pl
from jax.experimental.pallas import tpu as pltpu

F32 = jnp.float32
BF16 = jnp.bfloat16

GRID_W = 64
HEAD_DIM = 64
ROPE_THETA = 10000.0
RMS_EPS = 1e-6
H_A = 8
NOPE_A = 64
ROPE_A = 32
V_A = 64
Q_LORA = 384
KV_LORA = 256
H_B = 8
KV_B = 2
G_B = H_B // KV_B
H_C = 16
WIN_R = 8
WIN_C = 16
N_GROUPS = 4
EXPERTS_PER_GROUP = 8
N_EXPERTS = N_GROUPS * EXPERTS_PER_GROUP
D_EXPERT = 512

LANE = 128
TM = 256
MOE_BM = 512
NA_ROWS = TM // GRID_W
NEG = -1e30
LOG2E = 1.4426950408889634
FLASH_TK = 1280
FLASH_HEADS = 2
DMA_UNROLL = 8
VMEM_LIMIT = 48 * 1024 * 1024

_C_CQ = 0
_C_CKV = _C_CQ + Q_LORA
_C_KR = _C_CKV + KV_LORA
_C_GQ = _C_KR + LANE
_C_GK = _C_GQ + H_B * LANE
_C_END = _C_GK + KV_B * HEAD_DIM

VT_ROWS = LANE + 16


def _cparams(n_axes, vmem=VMEM_LIMIT):
    return pltpu.CompilerParams(dimension_semantics=("arbitrary",) * n_axes, vmem_limit_bytes=vmem)


def _rms(x):
    return x * lax.rsqrt(jnp.mean(x * x, axis=-1, keepdims=True) + RMS_EPS)


def _modulate(x, g, shift, scale):
    return (_rms(x) * g) * (1.0 + scale) + shift


def _silu(x):
    return x / (1.0 + jnp.exp(-x))


def _dot(a, b):
    return jnp.dot(a, b, preferred_element_type=F32)


def _dot_nt(a, b):
    return lax.dot_general(a, b, (((1,), (1,)), ((), ())), preferred_element_type=F32)


def _ada_kernel(c_ref, w_ref, b_ref, o_ref, *, n_vec):
    s = _silu(c_ref[...])
    w = w_ref[0]
    rows = []
    for r in range(8):
        if r < n_vec:
            rows.append(jnp.sum(w * s[:, r:r + 1], axis=0, keepdims=True) + b_ref[0])
        else:
            rows.append(jnp.zeros_like(b_ref[0]))
    o_ref[0] = jnp.concatenate(rows, axis=0)


def _ada_mod(cond_t, w_ada, b_ada, n_vec):
    depth, d, n6 = w_ada.shape
    tn = 512
    return pl.pallas_call(
        functools.partial(_ada_kernel, n_vec=n_vec),
        out_shape=jax.ShapeDtypeStruct((depth, 8, n6), F32),
        grid=(depth, n6 // tn),
        in_specs=[
            pl.BlockSpec((d, 8), lambda l, j: (0, 0)),
            pl.BlockSpec((1, d, tn), lambda l, j: (l, 0, j)),
            pl.BlockSpec((1, 1, tn), lambda l, j: (l, 0, j)),
        ],
        out_specs=pl.BlockSpec((1, 8, tn), lambda l, j: (l, 0, j)),
        compiler_params=_cparams(2),
    )(cond_t, w_ada, b_ada.reshape(depth, 1, n6))


def _rope(x, tab, sh):
    return (x * tab[:, 0:LANE]
            + pltpu.roll(x, LANE - sh, 1) * tab[:, LANE:2 * LANE]
            + pltpu.roll(x, sh, 1) * tab[:, 2 * LANE:3 * LANE])


def _with_ones_rows(vt):
    row = lax.broadcasted_iota(jnp.int32, vt.shape, 0)
    return jnp.where(row % VT_ROWS >= LANE, 1.0, vt)


def _ab_proj_kernel(x_ref, mod_ref, g_ref, win_ref, qn_ref, wq_ref, kvn_ref, wk_ref, wvt_ref, wgvt_ref,
                    gqn_ref, gkn_ref, ta_ref, tb_ref,
                    qa_ref, ka_ref, vat_ref, qb_ref, kb_ref, vbt_ref):
    mod = mod_ref[0]
    h = _modulate(x_ref[0], g_ref[...], mod[0:1], mod[1:2]).astype(BF16)
    p = _dot(h, win_ref[...])
    ta = ta_ref[...]
    tb = tb_ref[...]
    scale_a = (NOPE_A + ROPE_A) ** -0.5 * LOG2E
    scale_b = HEAD_DIM ** -0.5 * LOG2E

    cqn = (_rms(p[:, _C_CQ:_C_CKV]) * qn_ref[...]).astype(BF16)
    qa = _dot(cqn, wq_ref[...])
    ckvn = (_rms(p[:, _C_CKV:_C_KR]) * kvn_ref[...]).astype(BF16)
    ka = _dot(ckvn, wk_ref[...])
    vat_ref[0, 0] = _with_ones_rows(_dot_nt(wvt_ref[...], ckvn)).astype(BF16)
    kr = _rope(p[:, _C_KR:_C_GQ], ta, ROPE_A // 2)
    for hh in range(H_A):
        sl = slice(hh * LANE, (hh + 1) * LANE)
        qa_ref[0, :, sl] = (_rope(qa[:, sl], ta, ROPE_A // 2) * scale_a).astype(BF16)
        ka_ref[0, :, sl] = (ka[:, sl] + kr).astype(BF16)

    for hh in range(H_B):
        sl = slice(hh * LANE, (hh + 1) * LANE)
        xh = p[:, _C_GQ + hh * LANE:_C_GQ + (hh + 1) * LANE]
        r = lax.rsqrt(jnp.sum(xh * xh, axis=-1, keepdims=True) * (1.0 / HEAD_DIM) + RMS_EPS)
        qb_ref[0, :, sl] = (_rope(xh * r * gqn_ref[:, sl], tb, HEAD_DIM // 2) * scale_b).astype(BF16)
    gk = p[:, _C_GK:_C_END]
    lo = lax.broadcasted_iota(jnp.int32, gk.shape, 1) < HEAD_DIM
    sq = gk * gk
    s0 = jnp.sum(jnp.where(lo, sq, 0.0), axis=-1, keepdims=True)
    s1 = jnp.sum(jnp.where(lo, 0.0, sq), axis=-1, keepdims=True)
    r = jnp.where(lo, lax.rsqrt(s0 * (1.0 / HEAD_DIM) + RMS_EPS), lax.rsqrt(s1 * (1.0 / HEAD_DIM) + RMS_EPS))
    kb_ref[0] = _rope(gk * r * gkn_ref[...], tb, HEAD_DIM // 2).astype(BF16)
    vbt_ref[0, 0] = _with_ones_rows(_dot_nt(wgvt_ref[...], h)).astype(BF16)


def _ab_proj(xall, mod3, layer, g, w, tabs, n_lat_tiles, bsz):
    _, t, d = xall.shape
    nt = t // TM
    const = lambda b, i: (0, 0)
    tok = lambda b, i: (b, i, 0)

    def mod_map(b, i):
        return (layer * 8 + jnp.where(i >= n_lat_tiles, bsz, b), 0, 0)

    def full(a):
        return pl.BlockSpec(a.shape, const)

    def tok_out(n):
        return jax.ShapeDtypeStruct((bsz, t, n), BF16), pl.BlockSpec((1, TM, n), tok)

    def vt_out(rows):
        return (jax.ShapeDtypeStruct((bsz, nt, rows, TM), BF16),
                pl.BlockSpec((1, 1, rows, TM), lambda b, i: (b, i, 0, 0)))

    outs = [tok_out(H_A * LANE), tok_out(H_A * LANE), vt_out(H_A // 2 * VT_ROWS),
            tok_out(H_B * LANE), tok_out(KV_B * HEAD_DIM), vt_out(VT_ROWS)]
    return pl.pallas_call(
        _ab_proj_kernel,
        out_shape=[o[0] for o in outs],
        grid=(bsz, nt),
        in_specs=[
            pl.BlockSpec((1, TM, d), tok),
            pl.BlockSpec((1, 6, d), mod_map),
            full(g), full(w["win"]), full(w["qn"]), full(w["wq"]), full(w["kvn"]), full(w["wk"]),
            full(w["wvt"]), full(w["wgvt"]), full(w["gqn"]), full(w["gkn"]),
            pl.BlockSpec((TM, 3 * LANE), lambda b, i: (i, 0)),
            pl.BlockSpec((TM, 3 * LANE), lambda b, i: (i, 0)),
        ],
        out_specs=[o[1] for o in outs],
        compiler_params=_cparams(2),
    )(xall, mod3, g, w["win"], w["qn"], w["wq"], w["kvn"], w["wk"], w["wvt"], w["wgvt"], w["gqn"], w["gkn"],
      tabs[0], tabs[1])


def _flash_kernel(q_ref, k_ref, vt_ref, o_ref, st_ref, *, n_keys, tk, k_per_head):
    nh = FLASH_HEADS
    qs = [q_ref[0, :, a * LANE:(a + 1) * LANE] for a in range(nh)]
    tiles_per_chunk = tk // TM
    n_chunks = n_keys // tk

    def vt_rows(a):
        r0 = (a // 2) * VT_ROWS if k_per_head else 0
        return slice(r0, r0 + VT_ROWS)

    def scores(start, size, a):
        kc0 = a * LANE if k_per_head else 0
        return _dot_nt(k_ref[0, pl.ds(start, size), kc0:kc0 + LANE], qs[a])

    def update(st, vt, state):
        m, acc = state
        m_new = jnp.maximum(m, jnp.max(st, axis=0, keepdims=True))
        alpha = jnp.exp2(m - m_new)
        pexp = jnp.exp2(st - m_new).astype(BF16)
        return m_new, alpha * acc + _dot(vt, pexp)

    def qk_chunk(j, buf):
        for a in range(nh):
            st_ref[buf, a] = scores(pl.multiple_of(j * tk, tk), tk, a)

    def pv_chunk(j, buf, carry):
        vt = jnp.concatenate([vt_ref[0, j * tiles_per_chunk + u] for u in range(tiles_per_chunk)], axis=1)
        return tuple(update(st_ref[buf, a], vt[vt_rows(a)], carry[a]) for a in range(nh))

    init = (jnp.full((1, TM), NEG, F32), jnp.zeros((VT_ROWS, TM), F32))
    qk_chunk(0, 0)

    def pair(jj, carry):
        c0 = 2 * jj
        qk_chunk(c0 + 1, 1)
        carry = pv_chunk(c0, 0, carry)
        qk_chunk(jnp.minimum(c0 + 2, n_chunks - 1), 0)
        return pv_chunk(c0 + 1, 1, carry)

    full = lax.fori_loop(0, n_chunks // 2, pair, (init,) * nh)
    if n_chunks % 2:
        full = pv_chunk(n_chunks - 1, 0, full)
    hd = HEAD_DIM
    kv_lo = (pl.program_id(1) * nh) // G_B == 0
    outs = []
    for a, (_, acc) in enumerate(full):
        o = acc[0:LANE] / acc[LANE:LANE + 1]
        if k_per_head:
            outs.append(o[(a % 2) * hd:(a % 2 + 1) * hd])
        else:
            outs.append(jnp.where(kv_lo, o[0:hd], o[hd:2 * hd]))
    o_ref[0] = jnp.concatenate(outs, axis=0).T.astype(BF16)


def _flash(q, k, vt, s_len, k_per_head):
    bsz, t, qw = q.shape
    nh = FLASH_HEADS
    assert G_B % nh == 0 or nh % G_B == 0
    n_groups = qw // (nh * LANE)
    nt = t // TM
    n_lat = s_len // TM
    assert nt == n_lat + 1
    f = max(d for d in range(1, nt + 1) if nt % d == 0 and d * TM <= FLASH_TK)
    kw = nh * LANE if k_per_head else LANE
    vrows = nh // 2 * VT_ROWS if k_per_head else VT_ROWS
    kcol = (lambda g: g) if k_per_head else (lambda g: 0)

    def call(q_tile0, n_q_tiles, key_tile0, n_key_tiles, tk):
        n_keys = n_key_tiles * TM
        once = pl.Buffered(1)
        in_specs = [
            pl.BlockSpec((1, TM, nh * LANE), lambda b, g, i: (b, q_tile0 + i, g)),
            pl.BlockSpec((1, n_keys, kw), lambda b, g, i: (b, key_tile0 // n_key_tiles, kcol(g)),
                         pipeline_mode=once),
            pl.BlockSpec((1, n_key_tiles, vrows, TM), lambda b, g, i: (b, key_tile0 // n_key_tiles, kcol(g), 0),
                         pipeline_mode=once),
        ]
        return pl.pallas_call(
            functools.partial(_flash_kernel, n_keys=n_keys, tk=tk, k_per_head=k_per_head),
            out_shape=jax.ShapeDtypeStruct((bsz, n_q_tiles * TM, n_groups * nh * HEAD_DIM), BF16),
            grid=(bsz, n_groups, n_q_tiles),
            in_specs=in_specs,
            out_specs=pl.BlockSpec((1, TM, nh * HEAD_DIM), lambda b, g, i: (b, i, g)),
            scratch_shapes=[pltpu.VMEM((2, nh, tk, TM), F32)],
            compiler_params=_cparams(3),
        )(q, k, vt)

    return jnp.concatenate([call(0, n_lat, 0, nt, f * TM), call(n_lat, 1, n_lat, 1, TM)], axis=1)


def _na_proj_kernel(x_ref, mod_ref, g_ref, w_ref, q_ref, k_ref, v_ref):
    mod = mod_ref[0]
    h = _modulate(x_ref[0], g_ref[...], mod[0:1], mod[1:2]).astype(BF16)
    p = _dot(h, w_ref[...])
    cw = H_C * HEAD_DIM
    q_ref[0] = (p[:, 0:cw] * (HEAD_DIM ** -0.5 * LOG2E)).astype(BF16)
    k_ref[0] = p[:, cw:2 * cw].astype(BF16)
    v_ref[0] = p[:, 2 * cw:3 * cw].astype(BF16)


def _na_proj(xall, mod3, layer, g, w, n_lat_tiles, bsz):
    _, t, d = xall.shape
    cw = H_C * HEAD_DIM
    tok = lambda b, i: (b, i, 0)
    return pl.pallas_call(
        _na_proj_kernel,
        out_shape=[jax.ShapeDtypeStruct((bsz, t, cw), BF16)] * 3,
        grid=(bsz, t // TM),
        in_specs=[
            pl.BlockSpec((1, TM, d), tok),
            pl.BlockSpec((1, 6, d), lambda b, i: (layer * 8 + jnp.where(i >= n_lat_tiles, bsz, b), 0, 0)),
            pl.BlockSpec(g.shape, lambda b, i: (0, 0)),
            pl.BlockSpec(w.shape, lambda b, i: (0, 0)),
        ],
        out_specs=[pl.BlockSpec((1, TM, cw), tok)] * 3,
        compiler_params=_cparams(2),
    )(xall, mod3, g, w)


def _na_kernel(q_ref, k_ref, v_ref, bias_ref, o_ref, *, n_lat_tiles, s_len, ctx_len, rows_n):
    i = pl.program_id(2)
    nwin = WIN_R * GRID_W
    gw = GRID_W
    kc = k_ref[0, s_len:s_len + ctx_len, :]
    vc = v_ref[0, s_len:s_len + ctx_len, :]
    q2 = q_ref[0]
    lo = lax.broadcasted_iota(jnp.int32, (TM, LANE), 1) < HEAD_DIM
    zero = jnp.zeros_like(q2)
    qs = jnp.concatenate([jnp.where(lo, q2, zero), jnp.where(lo, zero, q2)], axis=0)
    sc_all = _dot_nt(qs, kc)

    def unstack(o):
        n = o.shape[0] // 2
        lo_n = lax.broadcasted_iota(jnp.int32, (n, LANE), 1) < HEAD_DIM
        return jnp.where(lo_n, o[0:n], o[n:2 * n])

    @pl.when(i < n_lat_tiles)
    def _():
        o_n, p_c, l_inv = [], [], []
        for rr in range(NA_ROWS):
            r = i * NA_ROWS + rr
            rs = jnp.clip(r - WIN_R // 2, 0, rows_n - WIN_R)
            var = r - rs
            start = pl.multiple_of(rs * gw, gw)
            kwin = k_ref[0, pl.ds(start, nwin), :]
            vwin = v_ref[0, pl.ds(start, nwin), :]
            rows = [slice(a * TM + rr * gw, a * TM + (rr + 1) * gw) for a in range(2)]
            q_r = jnp.concatenate([qs[rows[0]], qs[rows[1]]], axis=0)
            sn = _dot_nt(q_r, kwin) + jnp.concatenate([bias_ref[var, 0], bias_ref[var, 1]], axis=0)
            sc = jnp.concatenate([sc_all[rows[0]], sc_all[rows[1]]], axis=0)
            m = jnp.maximum(jnp.max(sn, axis=-1, keepdims=True), jnp.max(sc, axis=-1, keepdims=True))
            pn = jnp.exp2(sn - m)
            pc = jnp.exp2(sc - m)
            l_inv.append(1.0 / (jnp.sum(pn, axis=-1, keepdims=True) + jnp.sum(pc, axis=-1, keepdims=True)))
            o_n.append(_dot(pn.astype(BF16), vwin))
            p_c.append(pc.astype(BF16))
        o_c = _dot(jnp.concatenate(p_c, axis=0), vc)
        for rr in range(NA_ROWS):
            o = (o_n[rr] + o_c[rr * 2 * gw:(rr + 1) * 2 * gw]) * l_inv[rr]
            o_ref[0, rr * gw:(rr + 1) * gw, :] = unstack(o).astype(BF16)

    @pl.when(i >= n_lat_tiles)
    def _():
        m = jnp.max(sc_all, axis=-1, keepdims=True)
        pc = jnp.exp2(sc_all - m)
        l = jnp.sum(pc, axis=-1, keepdims=True)
        o_ref[0] = unstack(_dot(pc.astype(BF16), vc) / l).astype(BF16)


def _na_attention(q, k, v, bias, s_len):
    bsz, t, cw = q.shape
    n_pairs = cw // LANE
    rows_n = s_len // GRID_W
    assert rows_n >= WIN_R and GRID_W >= WIN_C
    tile = lambda b, p, i: (b, i, p)
    res = lambda b, p, i: (b, 0, p)
    return pl.pallas_call(
        functools.partial(_na_kernel, n_lat_tiles=s_len // TM, s_len=s_len, ctx_len=t - s_len, rows_n=rows_n),
        out_shape=jax.ShapeDtypeStruct((bsz, t, cw), BF16),
        grid=(bsz, n_pairs, t // TM),
        in_specs=[
            pl.BlockSpec((1, TM, LANE), tile),
            pl.BlockSpec((1, t, LANE), res),
            pl.BlockSpec((1, t, LANE), res),
            pl.BlockSpec((WIN_R, 2, GRID_W, WIN_R * GRID_W), lambda b, p, i: (0, p, 0, 0)),
        ],
        out_specs=pl.BlockSpec((1, TM, LANE), tile),
        compiler_params=_cparams(3),
    )(q, k, v, bias)


def _na_bias_table(rpb):
    rpb = rpb.astype(F32)
    rows = jnp.stack([rpb[:, WIN_R - 1 - v:2 * WIN_R - 1 - v, :] for v in range(WIN_R)], axis=0)
    rp = jnp.pad(rows, ((0, 0), (0, 0), (0, 0), (GRID_W, GRID_W)))
    off = GRID_W + WIN_C - 1
    vals = jnp.stack([rp[..., off - c:off - c + GRID_W] for c in range(GRID_W)], axis=2)
    col = np.arange(GRID_W)[:, None, None]
    kc = np.arange(GRID_W)[None, None, :]
    cs = np.clip(col - WIN_C // 2, 0, GRID_W - WIN_C)
    valid = np.broadcast_to((kc >= cs) & (kc < cs + WIN_C), (GRID_W, WIN_R, GRID_W))
    bias = jnp.where(valid[None, None], vals * LOG2E, NEG)
    return bias.reshape(WIN_R, rpb.shape[0], GRID_W, WIN_R * GRID_W)


def _post_kernel(x_ref, o1_ref, o2_ref, wo_ref, mod_ref, g_ref, wrh_ref, wrl_ref, br_ref, tri_ref,
                 xo_ref, route_ref, cnt_ref, base_ref):
    @pl.when((pl.program_id(0) == 0) & (pl.program_id(1) == 0))
    def _():
        base_ref[...] = jnp.zeros_like(base_ref)

    mod = mod_ref[0]
    half = wo_ref.shape[0] // 2
    o = _dot(o1_ref[0], wo_ref[0:half, :]) + _dot(o2_ref[0], wo_ref[half:2 * half, :])
    xn = x_ref[0] + mod[2:3] * o
    xo_ref[0] = xn

    h2 = _modulate(xn, g_ref[...], mod[3:4], mod[4:5])
    h_hi = h2.astype(BF16)
    h_lo = (h2 - h_hi.astype(F32)).astype(BF16)
    logits = (_dot(h_hi, wrh_ref[...]) + (_dot(h_lo, wrh_ref[...]) + _dot(h_hi, wrl_ref[...]))) + br_ref[...]
    half_l = LANE // 2

    def choose(lgt):
        lane = lax.broadcasted_iota(jnp.int32, lgt.shape, 1)
        isg = (lane >= N_EXPERTS) & (lane < N_EXPERTS + N_GROUPS)
        lg = jnp.where(isg, lgt, NEG)
        eg = jnp.where(isg, jnp.exp(lg - jnp.max(lg, axis=-1, keepdims=True)), 0.0)
        pg = eg / jnp.sum(eg, axis=-1, keepdims=True)
        p_gsel = jnp.max(pg, axis=-1, keepdims=True)
        gsel = jnp.min(jnp.where(isg & (pg == p_gsel), lane - N_EXPERTS, N_GROUPS), axis=-1, keepdims=True)
        sel = (lane < N_EXPERTS) & ((lane // EXPERTS_PER_GROUP) == gsel)
        le = jnp.where(sel, lgt, NEG)
        ee = jnp.where(sel, jnp.exp(le - jnp.max(le, axis=-1, keepdims=True)), 0.0)
        pe = ee / jnp.sum(ee, axis=-1, keepdims=True)
        pm = jnp.where(sel, pe, -1.0)
        p1 = jnp.max(pm, axis=-1, keepdims=True)
        i1 = jnp.min(jnp.where(pm == p1, lane, LANE), axis=-1, keepdims=True)
        pm2 = jnp.where(lane == i1, -2.0, pm)
        p2 = jnp.max(pm2, axis=-1, keepdims=True)
        i2 = jnp.min(jnp.where(pm2 == p2, lane, LANE), axis=-1, keepdims=True)
        den = p1 + p2
        return i1, i2, p_gsel * p1 / den, p_gsel * p2 / den

    i1, i2, w1, w2 = choose(logits)

    lane = lax.broadcasted_iota(jnp.int32, logits.shape, 1)
    oh1 = lane == i1
    oh2 = lane == i2 + half_l
    oh = jnp.where(oh1 | oh2, 1.0, 0.0)
    oh_sw = jnp.where((lane == i2) | (lane == i1 + half_l), 1.0, 0.0)
    prefix = _dot(tri_ref[...], oh.astype(BF16))
    cnt = jnp.sum(oh, axis=0, keepdims=True)
    cnt_sw = jnp.sum(oh_sw, axis=0, keepdims=True)
    base = base_ref[...]
    lane1 = lax.broadcasted_iota(jnp.int32, base.shape, 1)
    tot = prefix + base + jnp.where(lane1 >= half_l, cnt_sw, 0.0)
    r1 = jnp.sum(jnp.where(oh1, tot, 0.0), axis=-1, keepdims=True)
    r2 = jnp.sum(jnp.where(oh2, tot, 0.0), axis=-1, keepdims=True)
    new_base = base + cnt + cnt_sw
    base_ref[...] = new_base
    cnt_ref[...] = new_base

    cols = [i1.astype(F32), i2.astype(F32), r1, r2, w1, w2]
    route = jnp.zeros(logits.shape, F32)
    for c, val in enumerate(cols):
        route = jnp.where(lane == c, val, route)
    route_ref[0] = route


def _post(xall, o1, o2, o1_col, o2_col, wo, mod3, layer, g, wr, br, tri, n_lat_tiles, bsz):
    _, t, d = xall.shape
    half = wo.shape[0] // 2
    tok = lambda b, i: (b, i, 0)
    const = lambda b, i: (0, 0)
    wr_hi = wr.astype(BF16)
    wr_lo = (wr - wr_hi.astype(F32)).astype(BF16)
    return pl.pallas_call(
        _post_kernel,
        out_shape=[jax.ShapeDtypeStruct((bsz, t, d), F32), jax.ShapeDtypeStruct((bsz, t, LANE), F32),
                   jax.ShapeDtypeStruct((1, LANE), F32)],
        grid=(bsz, t // TM),
        in_specs=[
            pl.BlockSpec((1, TM, d), tok),
            pl.BlockSpec((1, TM, half), lambda b, i: (b, i, o1_col)),
            pl.BlockSpec((1, TM, half), lambda b, i: (b, i, o2_col)),
            pl.BlockSpec(wo.shape, const),
            pl.BlockSpec((1, 6, d), lambda b, i: (layer * 8 + jnp.where(i >= n_lat_tiles, bsz, b), 0, 0)),
            pl.BlockSpec(g.shape, const),
            pl.BlockSpec(wr.shape, const),
            pl.BlockSpec(wr.shape, const),
            pl.BlockSpec(br.shape, const),
            pl.BlockSpec(tri.shape, const),
        ],
        out_specs=[pl.BlockSpec((1, TM, d), tok), pl.BlockSpec((1, TM, LANE), tok),
                   pl.BlockSpec((1, LANE), const)],
        scratch_shapes=[pltpu.VMEM((1, LANE), F32)],
        compiler_params=_cparams(2),
    )(xall, o1, o2, wo, mod3, g, wr_hi, wr_lo, br, tri)


def _dispatch_kernel(dest_ref, x_ref, mod_ref, g_ref, xs_in_ref, xs_ref, hbuf, sem):
    del xs_in_ref
    mod = mod_ref[0]
    h = _modulate(x_ref[0], g_ref[...], mod[3:4], mod[4:5])
    bits = lax.bitcast_convert_type(h.astype(BF16).astype(F32), jnp.uint32)
    half = bits.shape[1] // 2
    hbuf[...] = (bits[:, :half] >> 16) | bits[:, half:]

    def row_copy(j, dst):
        return pltpu.make_async_copy(hbuf.at[pl.ds(j, 1)], xs_ref.at[pl.ds(dst, 1)], sem)

    def issue(j, c):
        for k in range(2):
            row_copy(j, dest_ref[0, 0, k * TM + j]).start()
        return c

    lax.fori_loop(0, TM, issue, 0, unroll=DMA_UNROLL)

    def drain(j, c):
        for k in range(2):
            row_copy(j, dest_ref[0, 0, k * TM + j]).wait()
        return c

    lax.fori_loop(0, TM, drain, 0, unroll=DMA_UNROLL)


def _dispatch(dest, xall, mod3, layer, g, total, n_lat_tiles, bsz):
    _, t, d = xall.shape
    nt = t // TM
    xs0 = jnp.zeros((total, d // 2), jnp.uint32)
    return pl.pallas_call(
        _dispatch_kernel,
        out_shape=jax.ShapeDtypeStruct((total, d // 2), jnp.uint32),
        grid=(bsz, nt),
        in_specs=[
            pl.BlockSpec((1, 1, 2 * TM), lambda b, i: (b * nt + i, 0, 0), memory_space=pltpu.SMEM),
            pl.BlockSpec((1, TM, d), lambda b, i: (b, i, 0)),
            pl.BlockSpec((1, 6, d), lambda b, i: (layer * 8 + jnp.where(i >= n_lat_tiles, bsz, b), 0, 0)),
            pl.BlockSpec(g.shape, lambda b, i: (0, 0)),
            pl.BlockSpec(memory_space=pl.ANY),
        ],
        out_specs=pl.BlockSpec(memory_space=pl.ANY),
        scratch_shapes=[pltpu.VMEM((TM, d // 2), jnp.uint32), pltpu.SemaphoreType.DMA(())],
        input_output_aliases={4: 0},
        compiler_params=_cparams(2),
    )(dest, xall, mod3, g, xs0)


def _expert_kernel(be_ref, nb_ref, xs_ref, wg_ref, wu_ref, wd_ref, ys_ref, wgb, wub, wdb):
    i = pl.program_id(0)
    changed = (i == 0) | (be_ref[i] != be_ref[jnp.maximum(i - 1, 0)])

    @pl.when(changed)
    def _():
        wgb[...] = wg_ref[0, 0].astype(BF16)
        wub[...] = wu_ref[0, 0].astype(BF16)
        wdb[...] = wd_ref[0, 0].astype(BF16)

    @pl.when(i < nb_ref[0])
    def _():
        hrows = MOE_BM // 2
        gu = []
        for hh in range(2):
            pk = xs_ref[hh * hrows:(hh + 1) * hrows, :]
            x = jnp.concatenate([lax.bitcast_convert_type(pk << 16, F32),
                                 lax.bitcast_convert_type(pk & jnp.uint32(0xFFFF0000), F32)],
                                axis=1).astype(BF16)
            gu.append((_dot(x, wgb[...]), _dot(x, wub[...])))
        for hh in range(2):
            a = _silu(gu[hh][0]) * gu[hh][1]
            ys_ref[hh * hrows:(hh + 1) * hrows, :] = _dot(a.astype(BF16), wdb[...])

    @pl.when(i >= nb_ref[0])
    def _():
        ys_ref[...] = jnp.zeros_like(ys_ref)


def _experts(blk_e, nb_used, xs, w_gate, w_up, w_down, layer):
    total = xs.shape[0]
    d, de = w_gate.shape[-2:]
    nblk = total // MOE_BM
    return pl.pallas_call(
        _expert_kernel,
        out_shape=jax.ShapeDtypeStruct((total, d), F32),
        grid_spec=pltpu.PrefetchScalarGridSpec(
            num_scalar_prefetch=2,
            grid=(nblk,),
            in_specs=[
                pl.BlockSpec((MOE_BM, d // 2), lambda i, be, nb: (i, 0)),
                pl.BlockSpec((1, 1, d, de), lambda i, be, nb: (layer, be[i], 0, 0)),
                pl.BlockSpec((1, 1, d, de), lambda i, be, nb: (layer, be[i], 0, 0)),
                pl.BlockSpec((1, 1, de, d), lambda i, be, nb: (layer, be[i], 0, 0)),
            ],
            out_specs=pl.BlockSpec((MOE_BM, d), lambda i, be, nb: (i, 0)),
            scratch_shapes=[pltpu.VMEM((d, de), BF16), pltpu.VMEM((d, de), BF16), pltpu.VMEM((de, d), BF16)],
        ),
        compiler_params=_cparams(1),
    )(blk_e, nb_used, xs, w_gate, w_up, w_down)


def _combine_kernel(dest_ref, x_ref, route_ref, mod_ref, gf_ref, ys_ref, o_ref, ybuf, sem, *, final):
    def row_copy(j, k, src):
        return pltpu.make_async_copy(ys_ref.at[pl.ds(src, 1)], ybuf.at[pl.ds(k * TM + j, 1)], sem)

    def issue(j, c):
        for k in range(2):
            row_copy(j, k, dest_ref[0, 0, k * TM + j]).start()
        return c

    lax.fori_loop(0, TM, issue, 0, unroll=DMA_UNROLL)

    def drain(j, c):
        for k in range(2):
            row_copy(j, k, dest_ref[0, 0, k * TM + j]).wait()
        return c

    lax.fori_loop(0, TM, drain, 0, unroll=DMA_UNROLL)

    route = route_ref[0]
    y = ybuf[0:TM, :] * route[:, 4:5] + ybuf[TM:2 * TM, :] * route[:, 5:6]
    xn = x_ref[0] + mod_ref[0][5:6] * y
    if final:
        xn = _rms(xn) * gf_ref[...]
    o_ref[0] = xn


def _combine(dest, xall, route, mod3, layer, gf, ys, t_out, final, n_lat_tiles, bsz):
    _, t, d = xall.shape
    nt = t // TM
    tok = lambda b, i: (b, i, 0)
    return pl.pallas_call(
        functools.partial(_combine_kernel, final=final),
        out_shape=jax.ShapeDtypeStruct((bsz, t_out, d), F32),
        grid=(bsz, t_out // TM),
        in_specs=[
            pl.BlockSpec((1, 1, 2 * TM), lambda b, i: (b * nt + i, 0, 0), memory_space=pltpu.SMEM),
            pl.BlockSpec((1, TM, d), tok),
            pl.BlockSpec((1, TM, LANE), tok),
            pl.BlockSpec((1, 6, d), lambda b, i: (layer * 8 + jnp.where(i >= n_lat_tiles, bsz, b), 0, 0)),
            pl.BlockSpec(gf.shape, lambda b, i: (0, 0)),
            pl.BlockSpec(memory_space=pl.ANY),
        ],
        out_specs=pl.BlockSpec((1, TM, d), tok),
        scratch_shapes=[pltpu.VMEM((2 * TM, d), F32), pltpu.SemaphoreType.DMA(())],
        compiler_params=_cparams(2),
    )(dest, xall, route, mod3, gf, ys)


def _deint(n):
    return np.concatenate([np.arange(0, n, 2), np.arange(1, n, 2)])


def _rope_tables(s_len, ctx_len):
    tpos = jnp.arange(s_len, dtype=jnp.int32)
    r = (tpos // GRID_W).astype(F32)
    col = (tpos % GRID_W).astype(F32)

    def cos_sin(rot):
        n = rot // 4
        inv = ROPE_THETA ** (-jnp.arange(n, dtype=F32) / n)
        ang = jnp.concatenate([r[:, None] * inv, col[:, None] * inv], axis=-1)
        return jnp.cos(ang), jnp.sin(ang)

    def finish(cos_t, sinm_t, sinp_t, cos_ctx):
        zc = jnp.zeros((ctx_len, LANE), F32)
        return jnp.concatenate([
            jnp.concatenate([cos_t, sinm_t, sinp_t], axis=1),
            jnp.concatenate([cos_ctx, zc, zc], axis=1)], axis=0)

    ca, sa = cos_sin(ROPE_A)
    ha = ROPE_A // 2
    one = jnp.ones((s_len, NOPE_A), F32)
    z = lambda n: jnp.zeros((s_len, n), F32)
    tab_a = finish(
        jnp.concatenate([one, ca, ca, z(LANE - NOPE_A - ROPE_A)], axis=1),
        jnp.concatenate([z(NOPE_A), -sa, z(LANE - NOPE_A - ha)], axis=1),
        jnp.concatenate([z(NOPE_A + ha), sa, z(LANE - NOPE_A - ROPE_A)], axis=1),
        jnp.concatenate([jnp.ones((ctx_len, NOPE_A + ROPE_A), F32),
                         jnp.zeros((ctx_len, LANE - NOPE_A - ROPE_A), F32)], axis=1))
    cb, sb = cos_sin(HEAD_DIM)
    hb = HEAD_DIM // 2
    tab_b = finish(
        jnp.concatenate([cb, cb, cb, cb], axis=1),
        jnp.concatenate([-sb, z(hb), -sb, z(hb)], axis=1),
        jnp.concatenate([z(hb), sb, z(hb), sb], axis=1),
        jnp.ones((ctx_len, LANE), F32))
    return tab_a, tab_b


def _ab_weights(w_in, q_norm, w_q_up, kv_norm, w_kv_up, gq_norm, gk_norm):
    d = w_in.shape[0]
    o0 = Q_LORA
    o1 = o0 + KV_LORA
    o2 = o1 + ROPE_A
    o3 = o2 + H_B * HEAD_DIM
    o4 = o3 + KV_B * HEAD_DIM
    da = _deint(ROPE_A)
    db = _deint(HEAD_DIM)
    zeros = lambda *s: jnp.zeros(s, w_in.dtype)
    kr = jnp.concatenate([zeros(d, NOPE_A), w_in[:, o1:o2][:, da], zeros(d, LANE - NOPE_A - ROPE_A)], axis=1)
    gq = w_in[:, o2:o3].reshape(d, H_B, HEAD_DIM)[:, :, db]
    gq_blocks = []
    gqn_blocks = []
    gn = gq_norm[db]
    zn = jnp.zeros((HEAD_DIM,), gq_norm.dtype)
    for h in range(H_B):
        first = (h // G_B) == 0
        gq_blocks += [gq[:, h], zeros(d, HEAD_DIM)] if first else [zeros(d, HEAD_DIM), gq[:, h]]
        gqn_blocks += [gn, zn] if first else [zn, gn]
    gk = w_in[:, o3:o4].reshape(d, KV_B, HEAD_DIM)[:, :, db].reshape(d, KV_B * HEAD_DIM)
    win = jnp.concatenate([w_in[:, :o1], kr] + gq_blocks + [gk], axis=1).astype(BF16)
    assert win.shape[1] == _C_END
    pad_rows = VT_ROWS - LANE
    wgvt = jnp.concatenate([w_in[:, o4:].T, jnp.zeros((pad_rows, d), w_in.dtype)], axis=0).astype(BF16)
    wq = w_q_up.reshape(Q_LORA, H_A, NOPE_A + ROPE_A)
    wq = jnp.concatenate([wq[:, :, :NOPE_A], wq[:, :, NOPE_A:][:, :, da],
                          jnp.zeros((Q_LORA, H_A, LANE - NOPE_A - ROPE_A), w_q_up.dtype)], axis=2)
    wkv = w_kv_up.reshape(KV_LORA, H_A, NOPE_A + V_A)
    wk = jnp.concatenate([wkv[:, :, :NOPE_A], jnp.zeros((KV_LORA, H_A, LANE - NOPE_A), w_kv_up.dtype)], axis=2)
    wvt = wkv[:, :, NOPE_A:].reshape(KV_LORA, H_A // 2, 2 * V_A).transpose(1, 2, 0)
    wvt = jnp.concatenate([wvt, jnp.zeros((H_A // 2, pad_rows, KV_LORA), w_kv_up.dtype)], axis=1)
    return {
        "win": win,
        "qn": q_norm.reshape(1, -1),
        "wq": wq.reshape(Q_LORA, H_A * LANE).astype(BF16),
        "kvn": kv_norm.reshape(1, -1),
        "wk": wk.reshape(KV_LORA, H_A * LANE).astype(BF16),
        "wvt": wvt.reshape(H_A // 2 * VT_ROWS, KV_LORA).astype(BF16),
        "wgvt": wgvt,
        "gqn": jnp.concatenate(gqn_blocks).reshape(1, -1),
        "gkn": jnp.tile(gk_norm[db], KV_B).reshape(1, -1),
    }


def _moe_layout(route, cnt, bsz, nt, total):
    counts = cnt[0, :N_EXPERTS].astype(jnp.int32)
    padded = ((counts + MOE_BM - 1) // MOE_BM) * MOE_BM
    ends = jnp.cumsum(padded)
    pad_start = ends - padded
    e = route[..., 0:2].astype(jnp.int32)
    rank = route[..., 2:4].astype(jnp.int32)
    eid = jnp.arange(N_EXPERTS, dtype=jnp.int32)
    dest = jnp.sum(jnp.where(e[..., None] == eid, pad_start, 0), axis=-1) + rank
    dest = dest.reshape(bsz, nt, TM, 2).transpose(0, 1, 3, 2).reshape(bsz * nt, 1, 2 * TM)
    nblk = total // MOE_BM
    blk_start = jnp.arange(nblk, dtype=jnp.int32) * MOE_BM
    blk_e = jnp.sum((ends[None, :] <= blk_start[:, None]).astype(jnp.int32), axis=-1)
    blk_e = jnp.clip(blk_e, 0, N_EXPERTS - 1)
    nb_used = (ends[-1] // MOE_BM).astype(jnp.int32).reshape(1)
    return dest, blk_e, nb_used


def kernel(x, c, ctx, c_ctx, w_ada, b_ada, norm_mix, norm_ffn, norm_final, ab_w_in, ab_w_out, mla_q_norm,
           mla_w_q_up, mla_kv_norm, mla_w_kv_up, gqa_q_norm, gqa_k_norm, na_w_in, na_w_out, na_rpb, moe_w_rg,
           moe_b_rg, moe_w_re, moe_b_re, moe_w_gate, moe_w_up, moe_w_down):
    bsz, s_len, d = x.shape
    ctx_len = ctx.shape[1]
    depth = w_ada.shape[0]
    t = s_len + ctx_len
    assert s_len % TM == 0 and ctx_len == TM and s_len % GRID_W == 0 and bsz + 1 <= 8
    nt = t // TM
    n_lat_tiles = s_len // TM
    n_tok = bsz * t
    total = -(-(n_tok * 2 + N_EXPERTS * (MOE_BM - 1)) // MOE_BM) * MOE_BM

    cond = jnp.concatenate([c, c_ctx[None, :], jnp.zeros((8 - bsz - 1, d), c.dtype)], axis=0)
    mod = _ada_mod(cond.T, w_ada, b_ada, bsz + 1)
    mod3 = mod.reshape(depth * 8, 6, d)

    tabs = _rope_tables(s_len, ctx_len)
    tri = jnp.asarray(np.tril(np.ones((TM, TM), np.float32), -1), BF16)
    xall = jnp.concatenate([x, ctx], axis=1)

    for l in range(depth):
        last = l == depth - 1
        i = l // 2
        g_mix = norm_mix[l].reshape(1, d)
        g_ffn = norm_ffn[l].reshape(1, d)
        if l % 2 == 0:
            w = _ab_weights(ab_w_in[i], mla_q_norm[i], mla_w_q_up[i], mla_kv_norm[i], mla_w_kv_up[i],
                            gqa_q_norm[i], gqa_k_norm[i])
            qa, ka, va, qb, kb, vb = _ab_proj(xall, mod3, l, g_mix, w, tabs, n_lat_tiles, bsz)
            o1 = _flash(qa, ka, va, s_len, True)
            o2 = _flash(qb, kb, vb, s_len, False)
            o1_col = o2_col = 0
            wo = ab_w_out[i].astype(BF16)
        else:
            q, k, v = _na_proj(xall, mod3, l, g_mix, na_w_in[i].astype(BF16), n_lat_tiles, bsz)
            o1 = o2 = _na_attention(q, k, v, _na_bias_table(na_rpb[i]), s_len)
            o1_col, o2_col = 0, 1
            wo = na_w_out[i].astype(BF16)

        wr = jnp.concatenate([moe_w_re[l], moe_w_rg[l],
                              jnp.zeros((d, LANE - N_EXPERTS - N_GROUPS), F32)], axis=1).astype(F32)
        br = jnp.concatenate([moe_b_re[l], moe_b_rg[l],
                              jnp.zeros((LANE - N_EXPERTS - N_GROUPS,), F32)]).reshape(1, LANE).astype(F32)
        xall, route, cnt = _post(xall, o1, o2, o1_col, o2_col, wo, mod3, l, g_ffn, wr, br, tri,
                                 n_lat_tiles, bsz)
        dest, blk_e, nb_used = _moe_layout(route, cnt, bsz, nt, total)
        xs = _dispatch(dest, xall, mod3, l, g_ffn, total, n_lat_tiles, bsz)
        ys = _experts(blk_e, nb_used, xs, moe_w_gate, moe_w_up, moe_w_down, l)
        t_out = s_len if last else t
        xall = _combine(dest, xall, route, mod3, l, norm_final.reshape(1, d), ys, t_out, last,
                        n_lat_tiles, bsz)
    return xall
```

```python
import functools

import numpy as np
import jax
import jax.numpy as jnp
from jax import lax
from jax.experimental import pallas as pl
from jax.experimental.pallas import tpu as pltpu

F32 = jnp.float32
BF16 = jnp.bfloat16

GRID_W = 64
HEAD_DIM = 64
ROPE_THETA = 10000.0
RMS_EPS = 1e-6
H_A = 8
NOPE_A = 64
ROPE_A = 32
V_A = 64
Q_LORA = 384
KV_LORA = 256
H_B = 8
KV_B = 2
G_B = H_B // KV_B
H_C = 16
WIN_R = 8
WIN_C = 16
N_GROUPS = 4
EXPERTS_PER_GROUP = 8
N_EXPERTS = N_GROUPS * EXPERTS_PER_GROUP
D_EXPERT = 512

LANE = 128
TM = 256
MOE_BM = 512
NA_ROWS = TM // GRID_W
NEG = -1e30
LOG2E = 1.4426950408889634
FLASH_TK = 1280
FLASH_HEADS = 2
DMA_UNROLL = 8
VMEM_LIMIT = 48 * 1024 * 1024

_C_CQ = 0
_C_CKV = _C_CQ + Q_LORA
_C_KR = _C_CKV + KV_LORA
_C_GQ = _C_KR + LANE
_C_GK = _C_GQ + H_B * LANE
_C_END = _C_GK + KV_B * HEAD_DIM

VT_ROWS = LANE + 16


def _cparams(n_axes, vmem=VMEM_LIMIT):
    return pltpu.CompilerParams(dimension_semantics=("arbitrary",) * n_axes, vmem_limit_bytes=vmem)


def _rms(x):
    return x * lax.rsqrt(jnp.mean(x * x, axis=-1, keepdims=True) + RMS_EPS)


def _modulate(x, g, shift, scale):
    return (_rms(x) * g) * (1.0 + scale) + shift


def _silu(x):
    return x / (1.0 + jnp.exp(-x))


def _dot(a, b):
    return jnp.dot(a, b, preferred_element_type=F32)


def _dot_nt(a, b):
    return lax.dot_general(a, b, (((1,), (1,)), ((), ())), preferred_element_type=F32)


def _ada_kernel(c_ref, w_ref, b_ref, o_ref, *, n_vec):
    s = _silu(c_ref[...])
    w = w_ref[0]
    rows = []
    for r in range(8):
        if r < n_vec:
            rows.append(jnp.sum(w * s[:, r:r + 1], axis=0, keepdims=True) + b_ref[0])
        else:
            rows.append(jnp.zeros_like(b_ref[0]))
    o_ref[0] = jnp.concatenate(rows, axis=0)


def _ada_mod(cond_t, w_ada, b_ada, n_vec):
    depth, d, n6 = w_ada.shape
    tn = 512
    return pl.pallas_call(
        functools.partial(_ada_kernel, n_vec=n_vec),
        out_shape=jax.ShapeDtypeStruct((depth, 8, n6), F32),
        grid=(depth, n6 // tn),
        in_specs=[
            pl.BlockSpec((d, 8), lambda l, j: (0, 0)),
            pl.BlockSpec((1, d, tn), lambda l, j: (l, 0, j)),
            pl.BlockSpec((1, 1, tn), lambda l, j: (l, 0, j)),
        ],
        out_specs=pl.BlockSpec((1, 8, tn), lambda l, j: (l, 0, j)),
        compiler_params=_cparams(2),
    )(cond_t, w_ada, b_ada.reshape(depth, 1, n6))


def _rope(x, tab, sh):
    return (x * tab[:, 0:LANE]
            + pltpu.roll(x, LANE - sh, 1) * tab[:, LANE:2 * LANE]
            + pltpu.roll(x, sh, 1) * tab[:, 2 * LANE:3 * LANE])


def _with_ones_rows(vt):
    row = lax.broadcasted_iota(jnp.int32, vt.shape, 0)
    return jnp.where(row % VT_ROWS >= LANE, 1.0, vt)


def _ab_proj_kernel(x_ref, mod_ref, g_ref, win_ref, qn_ref, wq_ref, kvn_ref, wk_ref, wvt_ref, wgvt_ref,
                    gqn_ref, gkn_ref, ta_ref, tb_ref,
                    qa_ref, ka_ref, vat_ref, qb_ref, kb_ref, vbt_ref):
    mod = mod_ref[0]
    h = _modulate(x_ref[0], g_ref[...], mod[0:1], mod[1:2]).astype(BF16)
    p = _dot(h, win_ref[...])
    ta = ta_ref[...]
    tb = tb_ref[...]
    scale_a = (NOPE_A + ROPE_A) ** -0.5 * LOG2E
    scale_b = HEAD_DIM ** -0.5 * LOG2E

    cqn = (_rms(p[:, _C_CQ:_C_CKV]) * qn_ref[...]).astype(BF16)
    qa = _dot(cqn, wq_ref[...])
    ckvn = (_rms(p[:, _C_CKV:_C_KR]) * kvn_ref[...]).astype(BF16)
    ka = _dot(ckvn, wk_ref[...])
    vat_ref[0, 0] = _with_ones_rows(_dot_nt(wvt_ref[...], ckvn)).astype(BF16)
    kr = _rope(p[:, _C_KR:_C_GQ], ta, ROPE_A // 2)
    for hh in range(H_A):
        sl = slice(hh * LANE, (hh + 1) * LANE)
        qa_ref[0, :, sl] = (_rope(qa[:, sl], ta, ROPE_A // 2) * scale_a).astype(BF16)
        ka_ref[0, :, sl] = (ka[:, sl] + kr).astype(BF16)

    for hh in range(H_B):
        sl = slice(hh * LANE, (hh + 1) * LANE)
        xh = p[:, _C_GQ + hh * LANE:_C_GQ + (hh + 1) * LANE]
        r = lax.rsqrt(jnp.sum(xh * xh, axis=-1, keepdims=True) * (1.0 / HEAD_DIM) + RMS_EPS)
        qb_ref[0, :, sl] = (_rope(xh * r * gqn_ref[:, sl], tb, HEAD_DIM // 2) * scale_b).astype(BF16)
    gk = p[:, _C_GK:_C_END]
    lo = lax.broadcasted_iota(jnp.int32, gk.shape, 1) < HEAD_DIM
    sq = gk * gk
    s0 = jnp.sum(jnp.where(lo, sq, 0.0), axis=-1, keepdims=True)
    s1 = jnp.sum(jnp.where(lo, 0.0, sq), axis=-1, keepdims=True)
    r = jnp.where(lo, lax.rsqrt(s0 * (1.0 / HEAD_DIM) + RMS_EPS), lax.rsqrt(s1 * (1.0 / HEAD_DIM) + RMS_EPS))
    kb_ref[0] = _rope(gk * r * gkn_ref[...], tb, HEAD_DIM // 2).astype(BF16)
    vbt_ref[0, 0] = _with_ones_rows(_dot_nt(wgvt_ref[...], h)).astype(BF16)


def _ab_proj(xall, mod3, layer, g, w, tabs, n_lat_tiles, bsz):
    _, t, d = xall.shape
    nt = t // TM
    const = lambda b, i: (0, 0)
    tok = lambda b, i: (b, i, 0)

    def mod_map(b, i):
        return (layer * 8 + jnp.where(i >= n_lat_tiles, bsz, b), 0, 0)

    def full(a):
        return pl.BlockSpec(a.shape, const)

    def tok_out(n):
        return jax.ShapeDtypeStruct((bsz, t, n), BF16), pl.BlockSpec((1, TM, n), tok)

    def vt_out(rows):
        return (jax.ShapeDtypeStruct((bsz, nt, rows, TM), BF16),
                pl.BlockSpec((1, 1, rows, TM), lambda b, i: (b, i, 0, 0)))

    outs = [tok_out(H_A * LANE), tok_out(H_A * LANE), vt_out(H_A // 2 * VT_ROWS),
            tok_out(H_B * LANE), tok_out(KV_B * HEAD_DIM), vt_out(VT_ROWS)]
    return pl.pallas_call(
        _ab_proj_kernel,
        out_shape=[o[0] for o in outs],
        grid=(bsz, nt),
        in_specs=[
            pl.BlockSpec((1, TM, d), tok),
            pl.BlockSpec((1, 6, d), mod_map),
            full(g), full(w["win"]), full(w["qn"]), full(w["wq"]), full(w["kvn"]), full(w["wk"]),
            full(w["wvt"]), full(w["wgvt"]), full(w["gqn"]), full(w["gkn"]),
            pl.BlockSpec((TM, 3 * LANE), lambda b, i: (i, 0)),
            pl.BlockSpec((TM, 3 * LANE), lambda b, i: (i, 0)),
        ],
        out_specs=[o[1] for o in outs],
        compiler_params=_cparams(2),
    )(xall, mod3, g, w["win"], w["qn"], w["wq"], w["kvn"], w["wk"], w["wvt"], w["wgvt"], w["gqn"], w["gkn"],
      tabs[0], tabs[1])


def _flash_kernel(q_ref, k_ref, vt_ref, o_ref, st_ref, *, n_keys, tk, k_per_head):
    nh = FLASH_HEADS
    qs = [q_ref[0, :, a * LANE:(a + 1) * LANE] for a in range(nh)]
    tiles_per_chunk = tk // TM
    n_chunks = n_keys // tk

    def vt_rows(a):
        r0 = (a // 2) * VT_ROWS if k_per_head else 0
        return slice(r0, r0 + VT_ROWS)

    def scores(start, size, a):
        kc0 = a * LANE if k_per_head else 0
        return _dot_nt(k_ref[0, pl.ds(start, size), kc0:kc0 + LANE], qs[a])

    def update(st, vt, state):
        m, acc = state
        m_new = jnp.maximum(m, jnp.max(st, axis=0, keepdims=True))
        alpha = jnp.exp2(m - m_new)
        pexp = jnp.exp2(st - m_new).astype(BF16)
        return m_new, alpha * acc + _dot(vt, pexp)

    def qk_chunk(j, buf):
        for a in range(nh):
            st_ref[buf, a] = scores(pl.multiple_of(j * tk, tk), tk, a)

    def pv_chunk(j, buf, carry):
        vt = jnp.concatenate([vt_ref[0, j * tiles_per_chunk + u] for u in range(tiles_per_chunk)], axis=1)
        return tuple(update(st_ref[buf, a], vt[vt_rows(a)], carry[a]) for a in range(nh))

    init = (jnp.full((1, TM), NEG, F32), jnp.zeros((VT_ROWS, TM), F32))
    qk_chunk(0, 0)

    def pair(jj, carry):
        c0 = 2 * jj
        qk_chunk(c0 + 1, 1)
        carry = pv_chunk(c0, 0, carry)
        qk_chunk(jnp.minimum(c0 + 2, n_chunks - 1), 0)
        return pv_chunk(c0 + 1, 1, carry)

    full = lax.fori_loop(0, n_chunks // 2, pair, (init,) * nh, unroll=3)
    if n_chunks % 2:
        full = pv_chunk(n_chunks - 1, 0, full)
    hd = HEAD_DIM
    kv_lo = (pl.program_id(1) * nh) // G_B == 0
    outs = []
    for a, (_, acc) in enumerate(full):
        o = acc[0:LANE] / acc[LANE:LANE + 1]
        if k_per_head:
            outs.append(o[(a % 2) * hd:(a % 2 + 1) * hd])
        else:
            outs.append(jnp.where(kv_lo, o[0:hd], o[hd:2 * hd]))
    o_ref[0] = jnp.concatenate(outs, axis=0).T.astype(BF16)


def _flash(q, k, vt, s_len, k_per_head):
    bsz, t, qw = q.shape
    nh = FLASH_HEADS
    assert G_B % nh == 0 or nh % G_B == 0
    n_groups = qw // (nh * LANE)
    nt = t // TM
    n_lat = s_len // TM
    assert nt == n_lat + 1
    f = max(d for d in range(1, nt + 1) if nt % d == 0 and d * TM <= FLASH_TK)
    kw = nh * LANE if k_per_head else LANE
    vrows = nh // 2 * VT_ROWS if k_per_head else VT_ROWS
    kcol = (lambda g: g) if k_per_head else (lambda g: 0)

    def call(q_tile0, n_q_tiles, key_tile0, n_key_tiles, tk):
        n_keys = n_key_tiles * TM
        once = pl.Buffered(1)
        in_specs = [
            pl.BlockSpec((1, TM, nh * LANE), lambda b, g, i: (b, q_tile0 + i, g)),
            pl.BlockSpec((1, n_keys, kw), lambda b, g, i: (b, key_tile0 // n_key_tiles, kcol(g)),
                         pipeline_mode=once),
            pl.BlockSpec((1, n_key_tiles, vrows, TM), lambda b, g, i: (b, key_tile0 // n_key_tiles, kcol(g), 0),
                         pipeline_mode=once),
        ]
        return pl.pallas_call(
            functools.partial(_flash_kernel, n_keys=n_keys, tk=tk, k_per_head=k_per_head),
            out_shape=jax.ShapeDtypeStruct((bsz, n_q_tiles * TM, n_groups * nh * HEAD_DIM), BF16),
            grid=(bsz, n_groups, n_q_tiles),
            in_specs=in_specs,
            out_specs=pl.BlockSpec((1, TM, nh * HEAD_DIM), lambda b, g, i: (b, i, g)),
            scratch_shapes=[pltpu.VMEM((2, nh, tk, TM), F32)],
            compiler_params=_cparams(3),
        )(q, k, vt)

    return jnp.concatenate([call(0, n_lat, 0, nt, f * TM), call(n_lat, 1, n_lat, 1, TM)], axis=1)


def _na_proj_kernel(x_ref, mod_ref, g_ref, w_ref, q_ref, k_ref, v_ref):
    mod = mod_ref[0]
    h = _modulate(x_ref[0], g_ref[...], mod[0:1], mod[1:2]).astype(BF16)
    p = _dot(h, w_ref[...])
    cw = H_C * HEAD_DIM
    q_ref[0] = (p[:, 0:cw] * (HEAD_DIM ** -0.5 * LOG2E)).astype(BF16)
    k_ref[0] = p[:, cw:2 * cw].astype(BF16)
    v_ref[0] = p[:, 2 * cw:3 * cw].astype(BF16)


def _na_proj(xall, mod3, layer, g, w, n_lat_tiles, bsz):
    _, t, d = xall.shape
    cw = H_C * HEAD_DIM
    tok = lambda b, i: (b, i, 0)
    return pl.pallas_call(
        _na_proj_kernel,
        out_shape=[jax.ShapeDtypeStruct((bsz, t, cw), BF16)] * 3,
        grid=(bsz, t // TM),
        in_specs=[
            pl.BlockSpec((1, TM, d), tok),
            pl.BlockSpec((1, 6, d), lambda b, i: (layer * 8 + jnp.where(i >= n_lat_tiles, bsz, b), 0, 0)),
            pl.BlockSpec(g.shape, lambda b, i: (0, 0)),
            pl.BlockSpec(w.shape, lambda b, i: (0, 0)),
        ],
        out_specs=[pl.BlockSpec((1, TM, cw), tok)] * 3,
        compiler_params=_cparams(2),
    )(xall, mod3, g, w)


def _na_kernel(q_ref, k_ref, v_ref, bias_ref, o_ref, *, n_lat_tiles, s_len, ctx_len, rows_n):
    i = pl.program_id(2)
    nwin = WIN_R * GRID_W
    gw = GRID_W
    kc = k_ref[0, s_len:s_len + ctx_len, :]
    vc = v_ref[0, s_len:s_len + ctx_len, :]
    q2 = q_ref[0]
    lo = lax.broadcasted_iota(jnp.int32, (TM, LANE), 1) < HEAD_DIM
    zero = jnp.zeros_like(q2)
    qs = jnp.concatenate([jnp.where(lo, q2, zero), jnp.where(lo, zero, q2)], axis=0)
    sc_all = _dot_nt(qs, kc)

    def unstack(o):
        n = o.shape[0] // 2
        lo_n = lax.broadcasted_iota(jnp.int32, (n, LANE), 1) < HEAD_DIM
        return jnp.where(lo_n, o[0:n], o[n:2 * n])

    @pl.when(i < n_lat_tiles)
    def _():
        o_n, p_c, l_inv = [], [], []
        for rr in range(NA_ROWS):
            r = i * NA_ROWS + rr
            rs = jnp.clip(r - WIN_R // 2, 0, rows_n - WIN_R)
            var = r - rs
            start = pl.multiple_of(rs * gw, gw)
            kwin = k_ref[0, pl.ds(start, nwin), :]
            vwin = v_ref[0, pl.ds(start, nwin), :]
            rows = [slice(a * TM + rr * gw, a * TM + (rr + 1) * gw) for a in range(2)]
            q_r = jnp.concatenate([qs[rows[0]], qs[rows[1]]], axis=0)
            sn = _dot_nt(q_r, kwin) + jnp.concatenate([bias_ref[var, 0], bias_ref[var, 1]], axis=0)
            sc = jnp.concatenate([sc_all[rows[0]], sc_all[rows[1]]], axis=0)
            m = jnp.maximum(jnp.max(sn, axis=-1, keepdims=True), jnp.max(sc, axis=-1, keepdims=True))
            pn = jnp.exp2(sn - m)
            pc = jnp.exp2(sc - m)
            l_inv.append(1.0 / (jnp.sum(pn, axis=-1, keepdims=True) + jnp.sum(pc, axis=-1, keepdims=True)))
            o_n.append(_dot(pn.astype(BF16), vwin))
            p_c.append(pc.astype(BF16))
        o_c = _dot(jnp.concatenate(p_c, axis=0), vc)
        for rr in range(NA_ROWS):
            o = (o_n[rr] + o_c[rr * 2 * gw:(rr + 1) * 2 * gw]) * l_inv[rr]
            o_ref[0, rr * gw:(rr + 1) * gw, :] = unstack(o).astype(BF16)

    @pl.when(i >= n_lat_tiles)
    def _():
        m = jnp.max(sc_all, axis=-1, keepdims=True)
        pc = jnp.exp2(sc_all - m)
        l = jnp.sum(pc, axis=-1, keepdims=True)
        o_ref[0] = unstack(_dot(pc.astype(BF16), vc) / l).astype(BF16)


def _na_attention(q, k, v, bias, s_len):
    bsz, t, cw = q.shape
    n_pairs = cw // LANE
    rows_n = s_len // GRID_W
    assert rows_n >= WIN_R and GRID_W >= WIN_C
    tile = lambda b, p, i: (b, i, p)
    res = lambda b, p, i: (b, 0, p)
    return pl.pallas_call(
        functools.partial(_na_kernel, n_lat_tiles=s_len // TM, s_len=s_len, ctx_len=t - s_len, rows_n=rows_n),
        out_shape=jax.ShapeDtypeStruct((bsz, t, cw), BF16),
        grid=(bsz, n_pairs, t // TM),
        in_specs=[
            pl.BlockSpec((1, TM, LANE), tile),
            pl.BlockSpec((1, t, LANE), res),
            pl.BlockSpec((1, t, LANE), res),
            pl.BlockSpec((WIN_R, 2, GRID_W, WIN_R * GRID_W), lambda b, p, i: (0, p, 0, 0)),
        ],
        out_specs=pl.BlockSpec((1, TM, LANE), tile),
        compiler_params=_cparams(3),
    )(q, k, v, bias)


def _na_bias_table(rpb):
    rpb = rpb.astype(F32)
    rows = jnp.stack([rpb[:, WIN_R - 1 - v:2 * WIN_R - 1 - v, :] for v in range(WIN_R)], axis=0)
    rp = jnp.pad(rows, ((0, 0), (0, 0), (0, 0), (GRID_W, GRID_W)))
    off = GRID_W + WIN_C - 1
    vals = jnp.stack([rp[..., off - c:off - c + GRID_W] for c in range(GRID_W)], axis=2)
    col = np.arange(GRID_W)[:, None, None]
    kc = np.arange(GRID_W)[None, None, :]
    cs = np.clip(col - WIN_C // 2, 0, GRID_W - WIN_C)
    valid = np.broadcast_to((kc >= cs) & (kc < cs + WIN_C), (GRID_W, WIN_R, GRID_W))
    bias = jnp.where(valid[None, None], vals * LOG2E, NEG)
    return bias.reshape(WIN_R, rpb.shape[0], GRID_W, WIN_R * GRID_W)


def _post_kernel(x_ref, o1_ref, o2_ref, wo_ref, mod_ref, g_ref, wrh_ref, wrl_ref, br_ref, tri_ref,
                 xo_ref, route_ref, cnt_ref, base_ref):
    @pl.when((pl.program_id(0) == 0) & (pl.program_id(1) == 0))
    def _():
        base_ref[...] = jnp.zeros_like(base_ref)

    mod = mod_ref[0]
    half = wo_ref.shape[0] // 2
    o = _dot(o1_ref[0], wo_ref[0:half, :]) + _dot(o2_ref[0], wo_ref[half:2 * half, :])
    xn = x_ref[0] + mod[2:3] * o
    xo_ref[0] = xn

    h2 = _modulate(xn, g_ref[...], mod[3:4], mod[4:5])
    h_hi = h2.astype(BF16)
    h_lo = (h2 - h_hi.astype(F32)).astype(BF16)
    logits = (_dot(h_hi, wrh_ref[...]) + (_dot(h_lo, wrh_ref[...]) + _dot(h_hi, wrl_ref[...]))) + br_ref[...]
    half_l = LANE // 2

    def choose(lgt):
        lane = lax.broadcasted_iota(jnp.int32, lgt.shape, 1)
        isg = (lane >= N_EXPERTS) & (lane < N_EXPERTS + N_GROUPS)
        lg = jnp.where(isg, lgt, NEG)
        eg = jnp.where(isg, jnp.exp(lg - jnp.max(lg, axis=-1, keepdims=True)), 0.0)
        pg = eg / jnp.sum(eg, axis=-1, keepdims=True)
        p_gsel = jnp.max(pg, axis=-1, keepdims=True)
        gsel = jnp.min(jnp.where(isg & (pg == p_gsel), lane - N_EXPERTS, N_GROUPS), axis=-1, keepdims=True)
        sel = (lane < N_EXPERTS) & ((lane // EXPERTS_PER_GROUP) == gsel)
        le = jnp.where(sel, lgt, NEG)
        ee = jnp.where(sel, jnp.exp(le - jnp.max(le, axis=-1, keepdims=True)), 0.0)
        pe = ee / jnp.sum(ee, axis=-1, keepdims=True)
        pm = jnp.where(sel, pe, -1.0)
        p1 = jnp.max(pm, axis=-1, keepdims=True)
        i1 = jnp.min(jnp.where(pm == p1, lane, LANE), axis=-1, keepdims=True)
        pm2 = jnp.where(lane == i1, -2.0, pm)
        p2 = jnp.max(pm2, axis=-1, keepdims=True)
        i2 = jnp.min(jnp.where(pm2 == p2, lane, LANE), axis=-1, keepdims=True)
        den = p1 + p2
        return i1, i2, p_gsel * p1 / den, p_gsel * p2 / den

    i1, i2, w1, w2 = choose(logits)

    lane = lax.broadcasted_iota(jnp.int32, logits.shape, 1)
    oh1 = lane == i1
    oh2 = lane == i2 + half_l
    oh = jnp.where(oh1 | oh2, 1.0, 0.0)
    oh_sw = jnp.where((lane == i2) | (lane == i1 + half_l), 1.0, 0.0)
    prefix = _dot(tri_ref[...], oh.astype(BF16))
    cnt = jnp.sum(oh, axis=0, keepdims=True)
    cnt_sw = jnp.sum(oh_sw, axis=0, keepdims=True)
    base = base_ref[...]
    lane1 = lax.broadcasted_iota(jnp.int32, base.shape, 1)
    tot = prefix + base + jnp.where(lane1 >= half_l, cnt_sw, 0.0)
    r1 = jnp.sum(jnp.where(oh1, tot, 0.0), axis=-1, keepdims=True)
    r2 = jnp.sum(jnp.where(oh2, tot, 0.0), axis=-1, keepdims=True)
    new_base = base + cnt + cnt_sw
    base_ref[...] = new_base
    cnt_ref[...] = new_base

    cols = [i1.astype(F32), i2.astype(F32), r1, r2, w1, w2]
    route = jnp.zeros(logits.shape, F32)
    for c, val in enumerate(cols):
        route = jnp.where(lane == c, val, route)
    route_ref[0] = route


def _post(xall, o1, o2, o1_col, o2_col, wo, mod3, layer, g, wr, br, tri, n_lat_tiles, bsz):
    _, t, d = xall.shape
    half = wo.shape[0] // 2
    tok = lambda b, i: (b, i, 0)
    const = lambda b, i: (0, 0)
    wr_hi = wr.astype(BF16)
    wr_lo = (wr - wr_hi.astype(F32)).astype(BF16)
    return pl.pallas_call(
        _post_kernel,
        out_shape=[jax.ShapeDtypeStruct((bsz, t, d), F32), jax.ShapeDtypeStruct((bsz, t, LANE), F32),
                   jax.ShapeDtypeStruct((1, LANE), F32)],
        grid=(bsz, t // TM),
        in_specs=[
            pl.BlockSpec((1, TM, d), tok),
            pl.BlockSpec((1, TM, half), lambda b, i: (b, i, o1_col)),
            pl.BlockSpec((1, TM, half), lambda b, i: (b, i, o2_col)),
            pl.BlockSpec(wo.shape, const),
            pl.BlockSpec((1, 6, d), lambda b, i: (layer * 8 + jnp.where(i >= n_lat_tiles, bsz, b), 0, 0)),
            pl.BlockSpec(g.shape, const),
            pl.BlockSpec(wr.shape, const),
            pl.BlockSpec(wr.shape, const),
            pl.BlockSpec(br.shape, const),
            pl.BlockSpec(tri.shape, const),
        ],
        out_specs=[pl.BlockSpec((1, TM, d), tok), pl.BlockSpec((1, TM, LANE), tok),
                   pl.BlockSpec((1, LANE), const)],
        scratch_shapes=[pltpu.VMEM((1, LANE), F32)],
        compiler_params=_cparams(2),
    )(xall, o1, o2, wo, mod3, g, wr_hi, wr_lo, br, tri)


def _dispatch_kernel(dest_ref, x_ref, mod_ref, g_ref, xs_in_ref, xs_ref, hbuf, sem):
    del xs_in_ref
    mod = mod_ref[0]
    h = _modulate(x_ref[0], g_ref[...], mod[3:4], mod[4:5])
    bits = lax.bitcast_convert_type(h.astype(BF16).astype(F32), jnp.uint32)
    half = bits.shape[1] // 2
    hbuf[...] = (bits[:, :half] >> 16) | bits[:, half:]

    def row_copy(j, dst):
        return pltpu.make_async_copy(hbuf.at[pl.ds(j, 1)], xs_ref.at[pl.ds(dst, 1)], sem)

    def issue(j, c):
        for k in range(2):
            row_copy(j, dest_ref[0, 0, k * TM + j]).start()
        return c

    lax.fori_loop(0, TM, issue, 0, unroll=DMA_UNROLL)

    def drain(j, c):
        for k in range(2):
            row_copy(j, dest_ref[0, 0, k * TM + j]).wait()
        return c

    lax.fori_loop(0, TM, drain, 0, unroll=DMA_UNROLL)


def _dispatch(dest, xall, mod3, layer, g, total, n_lat_tiles, bsz):
    _, t, d = xall.shape
    nt = t // TM
    xs0 = jnp.zeros((total, d // 2), jnp.uint32)
    return pl.pallas_call(
        _dispatch_kernel,
        out_shape=jax.ShapeDtypeStruct((total, d // 2), jnp.uint32),
        grid=(bsz, nt),
        in_specs=[
            pl.BlockSpec((1, 1, 2 * TM), lambda b, i: (b * nt + i, 0, 0), memory_space=pltpu.SMEM),
            pl.BlockSpec((1, TM, d), lambda b, i: (b, i, 0)),
            pl.BlockSpec((1, 6, d), lambda b, i: (layer * 8 + jnp.where(i >= n_lat_tiles, bsz, b), 0, 0)),
            pl.BlockSpec(g.shape, lambda b, i: (0, 0)),
            pl.BlockSpec(memory_space=pl.ANY),
        ],
        out_specs=pl.BlockSpec(memory_space=pl.ANY),
        scratch_shapes=[pltpu.VMEM((TM, d // 2), jnp.uint32), pltpu.SemaphoreType.DMA(())],
        input_output_aliases={4: 0},
        compiler_params=_cparams(2),
    )(dest, xall, mod3, g, xs0)


def _expert_kernel(be_ref, nb_ref, xs_ref, wg_ref, wu_ref, wd_ref, ys_ref, wgb, wub, wdb):
    i = pl.program_id(0)
    changed = (i == 0) | (be_ref[i] != be_ref[jnp.maximum(i - 1, 0)])

    @pl.when(changed)
    def _():
        wgb[...] = wg_ref[0, 0].astype(BF16)
        wub[...] = wu_ref[0, 0].astype(BF16)
        wdb[...] = wd_ref[0, 0].astype(BF16)

    @pl.when(i < nb_ref[0])
    def _():
        hrows = MOE_BM // 2
        gu = []
        for hh in range(2):
            pk = xs_ref[hh * hrows:(hh + 1) * hrows, :]
            x = jnp.concatenate([lax.bitcast_convert_type(pk << 16, F32),
                                 lax.bitcast_convert_type(pk & jnp.uint32(0xFFFF0000), F32)],
                                axis=1).astype(BF16)
            gu.append((_dot(x, wgb[...]), _dot(x, wub[...])))
        for hh in range(2):
            a = _silu(gu[hh][0]) * gu[hh][1]
            ys_ref[hh * hrows:(hh + 1) * hrows, :] = _dot(a.astype(BF16), wdb[...])

    @pl.when(i >= nb_ref[0])
    def _():
        ys_ref[...] = jnp.zeros_like(ys_ref)


def _experts(blk_e, nb_used, xs, w_gate, w_up, w_down, layer):
    total = xs.shape[0]
    d, de = w_gate.shape[-2:]
    nblk = total // MOE_BM
    return pl.pallas_call(
        _expert_kernel,
        out_shape=jax.ShapeDtypeStruct((total, d), F32),
        grid_spec=pltpu.PrefetchScalarGridSpec(
            num_scalar_prefetch=2,
            grid=(nblk,),
            in_specs=[
                pl.BlockSpec((MOE_BM, d // 2), lambda i, be, nb: (i, 0)),
                pl.BlockSpec((1, 1, d, de), lambda i, be, nb: (layer, be[i], 0, 0)),
                pl.BlockSpec((1, 1, d, de), lambda i, be, nb: (layer, be[i], 0, 0)),
                pl.BlockSpec((1, 1, de, d), lambda i, be, nb: (layer, be[i], 0, 0)),
            ],
            out_specs=pl.BlockSpec((MOE_BM, d), lambda i, be, nb: (i, 0)),
            scratch_shapes=[pltpu.VMEM((d, de), BF16), pltpu.VMEM((d, de), BF16), pltpu.VMEM((de, d), BF16)],
        ),
        compiler_params=_cparams(1),
    )(blk_e, nb_used, xs, w_gate, w_up, w_down)


def _combine_kernel(dest_ref, x_ref, route_ref, mod_ref, gf_ref, ys_ref, o_ref, ybuf, sem, *, final):
    def row_copy(j, k, src):
        return pltpu.make_async_copy(ys_ref.at[pl.ds(src, 1)], ybuf.at[pl.ds(k * TM + j, 1)], sem)

    def issue(j, c):
        for k in range(2):
            row_copy(j, k, dest_ref[0, 0, k * TM + j]).start()
        return c

    lax.fori_loop(0, TM, issue, 0, unroll=DMA_UNROLL)

    def drain(j, c):
        for k in range(2):
            row_copy(j, k, dest_ref[0, 0, k * TM + j]).wait()
        return c

    lax.fori_loop(0, TM, drain, 0, unroll=DMA_UNROLL)

    route = route_ref[0]
    y = ybuf[0:TM, :] * route[:, 4:5] + ybuf[TM:2 * TM, :] * route[:, 5:6]
    xn = x_ref[0] + mod_ref[0][5:6] * y
    if final:
        xn = _rms(xn) * gf_ref[...]
    o_ref[0] = xn


def _combine(dest, xall, route, mod3, layer, gf, ys, t_out, final, n_lat_tiles, bsz):
    _, t, d = xall.shape
    nt = t // TM
    tok = lambda b, i: (b, i, 0)
    return pl.pallas_call(
        functools.partial(_combine_kernel, final=final),
        out_shape=jax.ShapeDtypeStruct((bsz, t_out, d), F32),
        grid=(bsz, t_out // TM),
        in_specs=[
            pl.BlockSpec((1, 1, 2 * TM), lambda b, i: (b * nt + i, 0, 0), memory_space=pltpu.SMEM),
            pl.BlockSpec((1, TM, d), tok),
            pl.BlockSpec((1, TM, LANE), tok),
            pl.BlockSpec((1, 6, d), lambda b, i: (layer * 8 + jnp.where(i >= n_lat_tiles, bsz, b), 0, 0)),
            pl.BlockSpec(gf.shape, lambda b, i: (0, 0)),
            pl.BlockSpec(memory_space=pl.ANY),
        ],
        out_specs=pl.BlockSpec((1, TM, d), tok),
        scratch_shapes=[pltpu.VMEM((2 * TM, d), F32), pltpu.SemaphoreType.DMA(())],
        compiler_params=_cparams(2),
    )(dest, xall, route, mod3, gf, ys)


def _deint(n):
    return np.concatenate([np.arange(0, n, 2), np.arange(1, n, 2)])


def _rope_tables(s_len, ctx_len):
    tpos = jnp.arange(s_len, dtype=jnp.int32)
    r = (tpos // GRID_W).astype(F32)
    col = (tpos % GRID_W).astype(F32)

    def cos_sin(rot):
        n = rot // 4
        inv = ROPE_THETA ** (-jnp.arange(n, dtype=F32) / n)
        ang = jnp.concatenate([r[:, None] * inv, col[:, None] * inv], axis=-1)
        return jnp.cos(ang), jnp.sin(ang)

    def finish(cos_t, sinm_t, sinp_t, cos_ctx):
        zc = jnp.zeros((ctx_len, LANE), F32)
        return jnp.concatenate([
            jnp.concatenate([cos_t, sinm_t, sinp_t], axis=1),
            jnp.concatenate([cos_ctx, zc, zc], axis=1)], axis=0)

    ca, sa = cos_sin(ROPE_A)
    ha = ROPE_A // 2
    one = jnp.ones((s_len, NOPE_A), F32)
    z = lambda n: jnp.zeros((s_len, n), F32)
    tab_a = finish(
        jnp.concatenate([one, ca, ca, z(LANE - NOPE_A - ROPE_A)], axis=1),
        jnp.concatenate([z(NOPE_A), -sa, z(LANE - NOPE_A - ha)], axis=1),
        jnp.concatenate([z(NOPE_A + ha), sa, z(LANE - NOPE_A - ROPE_A)], axis=1),
        jnp.concatenate([jnp.ones((ctx_len, NOPE_A + ROPE_A), F32),
                         jnp.zeros((ctx_len, LANE - NOPE_A - ROPE_A), F32)], axis=1))
    cb, sb = cos_sin(HEAD_DIM)
    hb = HEAD_DIM // 2
    tab_b = finish(
        jnp.concatenate([cb, cb, cb, cb], axis=1),
        jnp.concatenate([-sb, z(hb), -sb, z(hb)], axis=1),
        jnp.concatenate([z(hb), sb, z(hb), sb], axis=1),
        jnp.ones((ctx_len, LANE), F32))
    return tab_a, tab_b


def _ab_weights(w_in, q_norm, w_q_up, kv_norm, w_kv_up, gq_norm, gk_norm):
    d = w_in.shape[0]
    o0 = Q_LORA
    o1 = o0 + KV_LORA
    o2 = o1 + ROPE_A
    o3 = o2 + H_B * HEAD_DIM
    o4 = o3 + KV_B * HEAD_DIM
    da = _deint(ROPE_A)
    db = _deint(HEAD_DIM)
    zeros = lambda *s: jnp.zeros(s, w_in.dtype)
    kr = jnp.concatenate([zeros(d, NOPE_A), w_in[:, o1:o2][:, da], zeros(d, LANE - NOPE_A - ROPE_A)], axis=1)
    gq = w_in[:, o2:o3].reshape(d, H_B, HEAD_DIM)[:, :, db]
    gq_blocks = []
    gqn_blocks = []
    gn = gq_norm[db]
    zn = jnp.zeros((HEAD_DIM,), gq_norm.dtype)
    for h in range(H_B):
        first = (h // G_B) == 0
        gq_blocks += [gq[:, h], zeros(d, HEAD_DIM)] if first else [zeros(d, HEAD_DIM), gq[:, h]]
        gqn_blocks += [gn, zn] if first else [zn, gn]
    gk = w_in[:, o3:o4].reshape(d, KV_B, HEAD_DIM)[:, :, db].reshape(d, KV_B * HEAD_DIM)
    win = jnp.concatenate([w_in[:, :o1], kr] + gq_blocks + [gk], axis=1).astype(BF16)
    assert win.shape[1] == _C_END
    pad_rows = VT_ROWS - LANE
    wgvt = jnp.concatenate([w_in[:, o4:].T, jnp.zeros((pad_rows, d), w_in.dtype)], axis=0).astype(BF16)
    wq = w_q_up.reshape(Q_LORA, H_A, NOPE_A + ROPE_A)
    wq = jnp.concatenate([wq[:, :, :NOPE_A], wq[:, :, NOPE_A:][:, :, da],
                          jnp.zeros((Q_LORA, H_A, LANE - NOPE_A - ROPE_A), w_q_up.dtype)], axis=2)
    wkv = w_kv_up.reshape(KV_LORA, H_A, NOPE_A + V_A)
    wk = jnp.concatenate([wkv[:, :, :NOPE_A], jnp.zeros((KV_LORA, H_A, LANE - NOPE_A), w_kv_up.dtype)], axis=2)
    wvt = wkv[:, :, NOPE_A:].reshape(KV_LORA, H_A // 2, 2 * V_A).transpose(1, 2, 0)
    wvt = jnp.concatenate([wvt, jnp.zeros((H_A // 2, pad_rows, KV_LORA), w_kv_up.dtype)], axis=1)
    return {
        "win": win,
        "qn": q_norm.reshape(1, -1),
        "wq": wq.reshape(Q_LORA, H_A * LANE).astype(BF16),
        "kvn": kv_norm.reshape(1, -1),
        "wk": wk.reshape(KV_LORA, H_A * LANE).astype(BF16),
        "wvt": wvt.reshape(H_A // 2 * VT_ROWS, KV_LORA).astype(BF16),
        "wgvt": wgvt,
        "gqn": jnp.concatenate(gqn_blocks).reshape(1, -1),
        "gkn": jnp.tile(gk_norm[db], KV_B).reshape(1, -1),
    }


def _moe_layout(route, cnt, bsz, nt, total):
    counts = cnt[0, :N_EXPERTS].astype(jnp.int32)
    padded = ((counts + MOE_BM - 1) // MOE_BM) * MOE_BM
    ends = jnp.cumsum(padded)
    pad_start = ends - padded
    e = route[..., 0:2].astype(jnp.int32)
    rank = route[..., 2:4].astype(jnp.int32)
    eid = jnp.arange(N_EXPERTS, dtype=jnp.int32)
    dest = jnp.sum(jnp.where(e[..., None] == eid, pad_start, 0), axis=-1) + rank
    dest = dest.reshape(bsz, nt, TM, 2).transpose(0, 1, 3, 2).reshape(bsz * nt, 1, 2 * TM)
    nblk = total // MOE_BM
    blk_start = jnp.arange(nblk, dtype=jnp.int32) * MOE_BM
    blk_e = jnp.sum((ends[None, :] <= blk_start[:, None]).astype(jnp.int32), axis=-1)
    blk_e = jnp.clip(blk_e, 0, N_EXPERTS - 1)
    nb_used = (ends[-1] // MOE_BM).astype(jnp.int32).reshape(1)
    return dest, blk_e, nb_used


def kernel(x, c, ctx, c_ctx, w_ada, b_ada, norm_mix, norm_ffn, norm_final, ab_w_in, ab_w_out, mla_q_norm,
           mla_w_q_up, mla_kv_norm, mla_w_kv_up, gqa_q_norm, gqa_k_norm, na_w_in, na_w_out, na_rpb, moe_w_rg,
           moe_b_rg, moe_w_re, moe_b_re, moe_w_gate, moe_w_up, moe_w_down):
    bsz, s_len, d = x.shape
    ctx_len = ctx.shape[1]
    depth = w_ada.shape[0]
    t = s_len + ctx_len
    assert s_len % TM == 0 and ctx_len == TM and s_len % GRID_W == 0 and bsz + 1 <= 8
    nt = t // TM
    n_lat_tiles = s_len // TM
    n_tok = bsz * t
    total = -(-(n_tok * 2 + N_EXPERTS * (MOE_BM - 1)) // MOE_BM) * MOE_BM

    cond = jnp.concatenate([c, c_ctx[None, :], jnp.zeros((8 - bsz - 1, d), c.dtype)], axis=0)
    mod = _ada_mod(cond.T, w_ada, b_ada, bsz + 1)
    mod3 = mod.reshape(depth * 8, 6, d)

    tabs = _rope_tables(s_len, ctx_len)
    tri = jnp.asarray(np.tril(np.ones((TM, TM), np.float32), -1), BF16)
    xall = jnp.concatenate([x, ctx], axis=1)

    for l in range(depth):
        last = l == depth - 1
        i = l // 2
        g_mix = norm_mix[l].reshape(1, d)
        g_ffn = norm_ffn[l].reshape(1, d)
        if l % 2 == 0:
            w = _ab_weights(ab_w_in[i], mla_q_norm[i], mla_w_q_up[i], mla_kv_norm[i], mla_w_kv_up[i],
                            gqa_q_norm[i], gqa_k_norm[i])
            qa, ka, va, qb, kb, vb = _ab_proj(xall, mod3, l, g_mix, w, tabs, n_lat_tiles, bsz)
            o1 = _flash(qa, ka, va, s_len, True)
            o2 = _flash(qb, kb, vb, s_len, False)
            o1_col = o2_col = 0
            wo = ab_w_out[i].astype(BF16)
        else:
            q, k, v = _na_proj(xall, mod3, l, g_mix, na_w_in[i].astype(BF16), n_lat_tiles, bsz)
            o1 = o2 = _na_attention(q, k, v, _na_bias_table(na_rpb[i]), s_len)
            o1_col, o2_col = 0, 1
            wo = na_w_out[i].astype(BF16)

        wr = jnp.concatenate([moe_w_re[l], moe_w_rg[l],
                              jnp.zeros((d, LANE - N_EXPERTS - N_GROUPS), F32)], axis=1).astype(F32)
        br = jnp.concatenate([moe_b_re[l], moe_b_rg[l],
                              jnp.zeros((LANE - N_EXPERTS - N_GROUPS,), F32)]).reshape(1, LANE).astype(F32)
        xall, route, cnt = _post(xall, o1, o2, o1_col, o2_col, wo, mod3, l, g_ffn, wr, br, tri,
                                 n_lat_tiles, bsz)
        dest, blk_e, nb_used = _moe_layout(route, cnt, bsz, nt, total)
        xs = _dispatch(dest, xall, mod3, l, g_ffn, total, n_lat_tiles, bsz)
        ys = _experts(blk_e, nb_used, xs, moe_w_gate, moe_w_up, moe_w_down, l)
        t_out = s_len if last else t
        xall = _combine(dest, xall, route, mod3, l, norm_final.reshape(1, d), ys, t_out, last,
                        n_lat_tiles, bsz)
    return xall
```

```python
import functools

import numpy as np
import jax
import jax.numpy as jnp
from jax import lax
from jax.experimental import pallas as pl
from jax.experimental.pallas import tpu as pltpu

F32 = jnp.float32
BF16 = jnp.bfloat16

GRID_W = 64
HEAD_DIM = 64
ROPE_THETA = 10000.0
RMS_EPS = 1e-6
H_A = 8
NOPE_A = 64
ROPE_A = 32
V_A = 64
Q_LORA = 384
KV_LORA = 256
H_B = 8
KV_B = 2
G_B = H_B // KV_B
H_C = 16
WIN_R = 8
WIN_C = 16
N_GROUPS = 4
EXPERTS_PER_GROUP = 8
N_EXPERTS = N_GROUPS * EXPERTS_PER_GROUP
D_EXPERT = 512

LANE = 128
TM = 256
MOE_BM = 512
NA_ROWS = TM // GRID_W
NEG = -1e30
LOG2E = 1.4426950408889634
FLASH_TK = 1280
FLASH_HEADS = 2
DMA_UNROLL = 8
VMEM_LIMIT = 48 * 1024 * 1024

_C_CQ = 0
_C_CKV = _C_CQ + Q_LORA
_C_KR = _C_CKV + KV_LORA
_C_GQ = _C_KR + LANE
_C_GK = _C_GQ + H_B * LANE
_C_END = _C_GK + KV_B * HEAD_DIM

VT_ROWS = LANE + 16


def _cparams(n_axes, vmem=VMEM_LIMIT):
    return pltpu.CompilerParams(dimension_semantics=("arbitrary",) * n_axes, vmem_limit_bytes=vmem)


def _rms(x):
    return x * lax.rsqrt(jnp.mean(x * x, axis=-1, keepdims=True) + RMS_EPS)


def _modulate(x, g, shift, scale):
    return (_rms(x) * g) * (1.0 + scale) + shift


def _silu(x):
    return x / (1.0 + jnp.exp(-x))


def _dot(a, b):
    return jnp.dot(a, b, preferred_element_type=F32)


def _dot_nt(a, b):
    return lax.dot_general(a, b, (((1,), (1,)), ((), ())), preferred_element_type=F32)


def _ada_kernel(c_ref, w_ref, b_ref, o_ref, *, n_vec):
    s = _silu(c_ref[...])
    w = w_ref[0]
    rows = []
    for r in range(8):
        if r < n_vec:
            rows.append(jnp.sum(w * s[:, r:r + 1], axis=0, keepdims=True) + b_ref[0])
        else:
            rows.append(jnp.zeros_like(b_ref[0]))
    o_ref[0] = jnp.concatenate(rows, axis=0)


def _ada_mod(cond_t, w_ada, b_ada, n_vec):
    depth, d, n6 = w_ada.shape
    tn = 512
    return pl.pallas_call(
        functools.partial(_ada_kernel, n_vec=n_vec),
        out_shape=jax.ShapeDtypeStruct((depth, 8, n6), F32),
        grid=(depth, n6 // tn),
        in_specs=[
            pl.BlockSpec((d, 8), lambda l, j: (0, 0)),
            pl.BlockSpec((1, d, tn), lambda l, j: (l, 0, j)),
            pl.BlockSpec((1, 1, tn), lambda l, j: (l, 0, j)),
        ],
        out_specs=pl.BlockSpec((1, 8, tn), lambda l, j: (l, 0, j)),
        compiler_params=_cparams(2),
    )(cond_t, w_ada, b_ada.reshape(depth, 1, n6))


def _rope(x, tab, sh):
    return (x * tab[:, 0:LANE]
            + pltpu.roll(x, LANE - sh, 1) * tab[:, LANE:2 * LANE]
            + pltpu.roll(x, sh, 1) * tab[:, 2 * LANE:3 * LANE])


def _with_ones_rows(vt):
    row = lax.broadcasted_iota(jnp.int32, vt.shape, 0)
    return jnp.where(row % VT_ROWS >= LANE, 1.0, vt)


def _ab_proj_kernel(x_ref, mod_ref, g_ref, win_ref, qn_ref, wq_ref, kvn_ref, wk_ref, wvt_ref, wgvt_ref,
                    gqn_ref, gkn_ref, ta_ref, tb_ref,
                    qa_ref, ka_ref, vat_ref, qb_ref, kb_ref, vbt_ref):
    mod = mod_ref[0]
    h = _modulate(x_ref[0], g_ref[...], mod[0:1], mod[1:2]).astype(BF16)
    p = _dot(h, win_ref[...])
    ta = ta_ref[...]
    tb = tb_ref[...]
    scale_a = (NOPE_A + ROPE_A) ** -0.5 * LOG2E
    scale_b = HEAD_DIM ** -0.5 * LOG2E

    cqn = (_rms(p[:, _C_CQ:_C_CKV]) * qn_ref[...]).astype(BF16)
    qa = _dot(cqn, wq_ref[...])
    ckvn = (_rms(p[:, _C_CKV:_C_KR]) * kvn_ref[...]).astype(BF16)
    ka = _dot(ckvn, wk_ref[...])
    vat_ref[0, 0] = _with_ones_rows(_dot_nt(wvt_ref[...], ckvn)).astype(BF16)
    kr = _rope(p[:, _C_KR:_C_GQ], ta, ROPE_A // 2)
    for hh in range(H_A):
        sl = slice(hh * LANE, (hh + 1) * LANE)
        qa_ref[0, :, sl] = (_rope(qa[:, sl], ta, ROPE_A // 2) * scale_a).astype(BF16)
        ka_ref[0, :, sl] = (ka[:, sl] + kr).astype(BF16)

    for hh in range(H_B):
        sl = slice(hh * LANE, (hh + 1) * LANE)
        xh = p[:, _C_GQ + hh * LANE:_C_GQ + (hh + 1) * LANE]
        r = lax.rsqrt(jnp.sum(xh * xh, axis=-1, keepdims=True) * (1.0 / HEAD_DIM) + RMS_EPS)
        qb_ref[0, :, sl] = (_rope(xh * r * gqn_ref[:, sl], tb, HEAD_DIM // 2) * scale_b).astype(BF16)
    gk = p[:, _C_GK:_C_END]
    lo = lax.broadcasted_iota(jnp.int32, gk.shape, 1) < HEAD_DIM
    sq = gk * gk
    s0 = jnp.sum(jnp.where(lo, sq, 0.0), axis=-1, keepdims=True)
    s1 = jnp.sum(jnp.where(lo, 0.0, sq), axis=-1, keepdims=True)
    r = jnp.where(lo, lax.rsqrt(s0 * (1.0 / HEAD_DIM) + RMS_EPS), lax.rsqrt(s1 * (1.0 / HEAD_DIM) + RMS_EPS))
    kb_ref[0] = _rope(gk * r * gkn_ref[...], tb, HEAD_DIM // 2).astype(BF16)
    vbt_ref[0, 0] = _with_ones_rows(_dot_nt(wgvt_ref[...], h)).astype(BF16)


def _ab_proj(xall, mod3, layer, g, w, tabs, n_lat_tiles, bsz):
    _, t, d = xall.shape
    nt = t // TM
    const = lambda b, i: (0, 0)
    tok = lambda b, i: (b, i, 0)

    def mod_map(b, i):
        return (layer * 8 + jnp.where(i >= n_lat_tiles, bsz, b), 0, 0)

    def full(a):
        return pl.BlockSpec(a.shape, const)

    def tok_out(n):
        return jax.ShapeDtypeStruct((bsz, t, n), BF16), pl.BlockSpec((1, TM, n), tok)

    def vt_out(rows):
        return (jax.ShapeDtypeStruct((bsz, nt, rows, TM), BF16),
                pl.BlockSpec((1, 1, rows, TM), lambda b, i: (b, i, 0, 0)))

    outs = [tok_out(H_A * LANE), tok_out(H_A * LANE), vt_out(H_A // 2 * VT_ROWS),
            tok_out(H_B * LANE), tok_out(KV_B * HEAD_DIM), vt_out(VT_ROWS)]
    return pl.pallas_call(
        _ab_proj_kernel,
        out_shape=[o[0] for o in outs],
        grid=(bsz, nt),
        in_specs=[
            pl.BlockSpec((1, TM, d), tok),
            pl.BlockSpec((1, 6, d), mod_map),
            full(g), full(w["win"]), full(w["qn"]), full(w["wq"]), full(w["kvn"]), full(w["wk"]),
            full(w["wvt"]), full(w["wgvt"]), full(w["gqn"]), full(w["gkn"]),
            pl.BlockSpec((TM, 3 * LANE), lambda b, i: (i, 0)),
            pl.BlockSpec((TM, 3 * LANE), lambda b, i: (i, 0)),
        ],
        out_specs=[o[1] for o in outs],
        compiler_params=_cparams(2),
    )(xall, mod3, g, w["win"], w["qn"], w["wq"], w["kvn"], w["wk"], w["wvt"], w["wgvt"], w["gqn"], w["gkn"],
      tabs[0], tabs[1])


def _flash_kernel(q_ref, k_ref, vt_ref, o_ref, st_ref, *, n_keys, tk, k_per_head):
    nh = FLASH_HEADS
    qs = [q_ref[0, :, a * LANE:(a + 1) * LANE] for a in range(nh)]
    tiles_per_chunk = tk // TM
    n_chunks = n_keys // tk

    def vt_rows(a):
        r0 = (a // 2) * VT_ROWS if k_per_head else 0
        return slice(r0, r0 + VT_ROWS)

    def scores(start, size, a):
        kc0 = a * LANE if k_per_head else 0
        return _dot_nt(k_ref[0, pl.ds(start, size), kc0:kc0 + LANE], qs[a])

    def update(st, vt, state):
        m, acc = state
        m_new = jnp.maximum(m, jnp.max(st, axis=0, keepdims=True))
        alpha = jnp.exp2(m - m_new)
        pexp = jnp.exp2(st - m_new).astype(BF16)
        return m_new, alpha * acc + _dot(vt, pexp)

    def qk_chunk(j, buf):
        for a in range(nh):
            st_ref[buf, a] = scores(pl.multiple_of(j * tk, tk), tk, a)

    def pv_chunk(j, buf, carry):
        vt = jnp.concatenate([vt_ref[0, j * tiles_per_chunk + u] for u in range(tiles_per_chunk)], axis=1)
        return tuple(update(st_ref[buf, a], vt[vt_rows(a)], carry[a]) for a in range(nh))

    init = (jnp.full((1, TM), NEG, F32), jnp.zeros((VT_ROWS, TM), F32))
    qk_chunk(0, 0)

    def pair(jj, carry):
        c0 = 2 * jj
        qk_chunk(c0 + 1, 1)
        carry = pv_chunk(c0, 0, carry)
        qk_chunk(jnp.minimum(c0 + 2, n_chunks - 1), 0)
        return pv_chunk(c0 + 1, 1, carry)

    full = lax.fori_loop(0, n_chunks // 2, pair, (init,) * nh, unroll=3)
    if n_chunks % 2:
        full = pv_chunk(n_chunks - 1, 0, full)
    hd = HEAD_DIM
    kv_lo = (pl.program_id(1) * nh) // G_B == 0
    outs = []
    for a, (_, acc) in enumerate(full):
        o = acc[0:LANE] / acc[LANE:LANE + 1]
        if k_per_head:
            outs.append(o[(a % 2) * hd:(a % 2 + 1) * hd])
        else:
            outs.append(jnp.where(kv_lo, o[0:hd], o[hd:2 * hd]))
    o_ref[0] = jnp.concatenate(outs, axis=0).T.astype(BF16)


def _flash(q, k, vt, s_len, k_per_head):
    bsz, t, qw = q.shape
    nh = FLASH_HEADS
    assert G_B % nh == 0 or nh % G_B == 0
    n_groups = qw // (nh * LANE)
    nt = t // TM
    n_lat = s_len // TM
    assert nt == n_lat + 1
    f = max(d for d in range(1, nt + 1) if nt % d == 0 and d * TM <= FLASH_TK)
    kw = nh * LANE if k_per_head else LANE
    vrows = nh // 2 * VT_ROWS if k_per_head else VT_ROWS
    kcol = (lambda g: g) if k_per_head else (lambda g: 0)

    def call(q_tile0, n_q_tiles, key_tile0, n_key_tiles, tk):
        n_keys = n_key_tiles * TM
        once = pl.Buffered(1)
        in_specs = [
            pl.BlockSpec((1, TM, nh * LANE), lambda b, g, i: (b, q_tile0 + i, g)),
            pl.BlockSpec((1, n_keys, kw), lambda b, g, i: (b, key_tile0 // n_key_tiles, kcol(g)),
                         pipeline_mode=once),
            pl.BlockSpec((1, n_key_tiles, vrows, TM), lambda b, g, i: (b, key_tile0 // n_key_tiles, kcol(g), 0),
                         pipeline_mode=once),
        ]
        return pl.pallas_call(
            functools.partial(_flash_kernel, n_keys=n_keys, tk=tk, k_per_head=k_per_head),
            out_shape=jax.ShapeDtypeStruct((bsz, n_q_tiles * TM, n_groups * nh * HEAD_DIM), BF16),
            grid=(bsz, n_groups, n_q_tiles),
            in_specs=in_specs,
            out_specs=pl.BlockSpec((1, TM, nh * HEAD_DIM), lambda b, g, i: (b, i, g)),
            scratch_shapes=[pltpu.VMEM((2, nh, tk, TM), F32)],
            compiler_params=_cparams(3),
        )(q, k, vt)

    return jnp.concatenate([call(0, n_lat, 0, nt, f * TM), call(n_lat, 1, n_lat, 1, TM)], axis=1)


def _na_proj_kernel(x_ref, mod_ref, g_ref, w_ref, q_ref, k_ref, v_ref):
    mod = mod_ref[0]
    h = _modulate(x_ref[0], g_ref[...], mod[0:1], mod[1:2]).astype(BF16)
    p = _dot(h, w_ref[...])
    cw = H_C * HEAD_DIM
    q_ref[0] = (p[:, 0:cw] * (HEAD_DIM ** -0.5 * LOG2E)).astype(BF16)
    k_ref[0] = p[:, cw:2 * cw].astype(BF16)
    v_ref[0] = p[:, 2 * cw:3 * cw].astype(BF16)


def _na_proj(xall, mod3, layer, g, w, n_lat_tiles, bsz):
    _, t, d = xall.shape
    cw = H_C * HEAD_DIM
    tok = lambda b, i: (b, i, 0)
    return pl.pallas_call(
        _na_proj_kernel,
        out_shape=[jax.ShapeDtypeStruct((bsz, t, cw), BF16)] * 3,
        grid=(bsz, t // TM),
        in_specs=[
            pl.BlockSpec((1, TM, d), tok),
            pl.BlockSpec((1, 6, d), lambda b, i: (layer * 8 + jnp.where(i >= n_lat_tiles, bsz, b), 0, 0)),
            pl.BlockSpec(g.shape, lambda b, i: (0, 0)),
            pl.BlockSpec(w.shape, lambda b, i: (0, 0)),
        ],
        out_specs=[pl.BlockSpec((1, TM, cw), tok)] * 3,
        compiler_params=_cparams(2),
    )(xall, mod3, g, w)


def _na_kernel(q_ref, k_ref, v_ref, bias_ref, o_ref, *, n_lat_tiles, s_len, ctx_len, rows_n):
    i = pl.program_id(2)
    nwin = WIN_R * GRID_W
    gw = GRID_W
    kc = k_ref[0, s_len:s_len + ctx_len, :]
    vc = v_ref[0, s_len:s_len + ctx_len, :]
    q2 = q_ref[0]
    lo = lax.broadcasted_iota(jnp.int32, (TM, LANE), 1) < HEAD_DIM
    zero = jnp.zeros_like(q2)
    qs = jnp.concatenate([jnp.where(lo, q2, zero), jnp.where(lo, zero, q2)], axis=0)
    sc_all = _dot_nt(qs, kc)

    def unstack(o):
        n = o.shape[0] // 2
        lo_n = lax.broadcasted_iota(jnp.int32, (n, LANE), 1) < HEAD_DIM
        return jnp.where(lo_n, o[0:n], o[n:2 * n])

    @pl.when(i < n_lat_tiles)
    def _():
        o_n, p_c, l_inv = [], [], []
        for rr in range(NA_ROWS):
            r = i * NA_ROWS + rr
            rs = jnp.clip(r - WIN_R // 2, 0, rows_n - WIN_R)
            var = r - rs
            start = pl.multiple_of(rs * gw, gw)
            kwin = k_ref[0, pl.ds(start, nwin), :]
            vwin = v_ref[0, pl.ds(start, nwin), :]
            rows = [slice(a * TM + rr * gw, a * TM + (rr + 1) * gw) for a in range(2)]
            q_r = jnp.concatenate([qs[rows[0]], qs[rows[1]]], axis=0)
            sn = _dot_nt(q_r, kwin) + jnp.concatenate([bias_ref[var, 0], bias_ref[var, 1]], axis=0)
            sc = jnp.concatenate([sc_all[rows[0]], sc_all[rows[1]]], axis=0)
            m = jnp.maximum(jnp.max(sn, axis=-1, keepdims=True), jnp.max(sc, axis=-1, keepdims=True))
            pn = jnp.exp2(sn - m)
            pc = jnp.exp2(sc - m)
            l_inv.append(1.0 / (jnp.sum(pn, axis=-1, keepdims=True) + jnp.sum(pc, axis=-1, keepdims=True)))
            o_n.append(_dot(pn.astype(BF16), vwin))
            p_c.append(pc.astype(BF16))
        o_c = _dot(jnp.concatenate(p_c, axis=0), vc)
        for rr in range(NA_ROWS):
            o = (o_n[rr] + o_c[rr * 2 * gw:(rr + 1) * 2 * gw]) * l_inv[rr]
            o_ref[0, rr * gw:(rr + 1) * gw, :] = unstack(o).astype(BF16)

    @pl.when(i >= n_lat_tiles)
    def _():
        m = jnp.max(sc_all, axis=-1, keepdims=True)
        pc = jnp.exp2(sc_all - m)
        l = jnp.sum(pc, axis=-1, keepdims=True)
        o_ref[0] = unstack(_dot(pc.astype(BF16), vc) / l).astype(BF16)


def _na_attention(q, k, v, bias, s_len):
    bsz, t, cw = q.shape
    n_pairs = cw // LANE
    rows_n = s_len // GRID_W
    assert rows_n >= WIN_R and GRID_W >= WIN_C
    tile = lambda b, p, i: (b, i, p)
    res = lambda b, p, i: (b, 0, p)
    return pl.pallas_call(
        functools.partial(_na_kernel, n_lat_tiles=s_len // TM, s_len=s_len, ctx_len=t - s_len, rows_n=rows_n),
        out_shape=jax.ShapeDtypeStruct((bsz, t, cw), BF16),
        grid=(bsz, n_pairs, t // TM),
        in_specs=[
            pl.BlockSpec((1, TM, LANE), tile),
            pl.BlockSpec((1, t, LANE), res),
            pl.BlockSpec((1, t, LANE), res),
            pl.BlockSpec((WIN_R, 2, GRID_W, WIN_R * GRID_W), lambda b, p, i: (0, p, 0, 0)),
        ],
        out_specs=pl.BlockSpec((1, TM, LANE), tile),
        compiler_params=_cparams(3),
    )(q, k, v, bias)


def _na_bias_table(rpb):
    rpb = rpb.astype(F32)
    rows = jnp.stack([rpb[:, WIN_R - 1 - v:2 * WIN_R - 1 - v, :] for v in range(WIN_R)], axis=0)
    rp = jnp.pad(rows, ((0, 0), (0, 0), (0, 0), (GRID_W, GRID_W)))
    off = GRID_W + WIN_C - 1
    vals = jnp.stack([rp[..., off - c:off - c + GRID_W] for c in range(GRID_W)], axis=2)
    col = np.arange(GRID_W)[:, None, None]
    kc = np.arange(GRID_W)[None, None, :]
    cs = np.clip(col - WIN_C // 2, 0, GRID_W - WIN_C)
    valid = np.broadcast_to((kc >= cs) & (kc < cs + WIN_C), (GRID_W, WIN_R, GRID_W))
    bias = jnp.where(valid[None, None], vals * LOG2E, NEG)
    return bias.reshape(WIN_R, rpb.shape[0], GRID_W, WIN_R * GRID_W)


def _post_kernel(x_ref, o1_ref, o2_ref, wo_ref, mod_ref, g_ref, wrh_ref, wrl_ref, br_ref, tri_ref,
                 xo_ref, route_ref, cnt_ref, base_ref):
    @pl.when((pl.program_id(0) == 0) & (pl.program_id(1) == 0))
    def _():
        base_ref[...] = jnp.zeros_like(base_ref)

    mod = mod_ref[0]
    half = wo_ref.shape[0] // 2
    o = _dot(o1_ref[0], wo_ref[0:half, :]) + _dot(o2_ref[0], wo_ref[half:2 * half, :])
    xn = x_ref[0] + mod[2:3] * o
    xo_ref[0] = xn

    h2 = _modulate(xn, g_ref[...], mod[3:4], mod[4:5])
    h_hi = h2.astype(BF16)
    h_lo = (h2 - h_hi.astype(F32)).astype(BF16)
    logits = (_dot(h_hi, wrh_ref[...]) + (_dot(h_lo, wrh_ref[...]) + _dot(h_hi, wrl_ref[...]))) + br_ref[...]
    half_l = LANE // 2

    def choose(lgt):
        lane = lax.broadcasted_iota(jnp.int32, lgt.shape, 1)
        isg = (lane >= N_EXPERTS) & (lane < N_EXPERTS + N_GROUPS)
        lg = jnp.where(isg, lgt, NEG)
        eg = jnp.where(isg, jnp.exp(lg - jnp.max(lg, axis=-1, keepdims=True)), 0.0)
        pg = eg / jnp.sum(eg, axis=-1, keepdims=True)
        p_gsel = jnp.max(pg, axis=-1, keepdims=True)
        gsel = jnp.min(jnp.where(isg & (pg == p_gsel), lane - N_EXPERTS, N_GROUPS), axis=-1, keepdims=True)
        sel = (lane < N_EXPERTS) & ((lane // EXPERTS_PER_GROUP) == gsel)
        le = jnp.where(sel, lgt, NEG)
        ee = jnp.where(sel, jnp.exp(le - jnp.max(le, axis=-1, keepdims=True)), 0.0)
        pe = ee / jnp.sum(ee, axis=-1, keepdims=True)
        pm = jnp.where(sel, pe, -1.0)
        p1 = jnp.max(pm, axis=-1, keepdims=True)
        i1 = jnp.min(jnp.where(pm == p1, lane, LANE), axis=-1, keepdims=True)
        pm2 = jnp.where(lane == i1, -2.0, pm)
        p2 = jnp.max(pm2, axis=-1, keepdims=True)
        i2 = jnp.min(jnp.where(pm2 == p2, lane, LANE), axis=-1, keepdims=True)
        den = p1 + p2
        return i1, i2, p_gsel * p1 / den, p_gsel * p2 / den

    i1, i2, w1, w2 = choose(logits)

    lane = lax.broadcasted_iota(jnp.int32, logits.shape, 1)
    oh1 = lane == i1
    oh2 = lane == i2 + half_l
    oh = jnp.where(oh1 | oh2, 1.0, 0.0)
    oh_sw = jnp.where((lane == i2) | (lane == i1 + half_l), 1.0, 0.0)
    prefix = _dot(tri_ref[...], oh.astype(BF16))
    cnt = jnp.sum(oh, axis=0, keepdims=True)
    cnt_sw = jnp.sum(oh_sw, axis=0, keepdims=True)
    base = base_ref[...]
    lane1 = lax.broadcasted_iota(jnp.int32, base.shape, 1)
    tot = prefix + base + jnp.where(lane1 >= half_l, cnt_sw, 0.0)
    r1 = jnp.sum(jnp.where(oh1, tot, 0.0), axis=-1, keepdims=True)
    r2 = jnp.sum(jnp.where(oh2, tot, 0.0), axis=-1, keepdims=True)
    new_base = base + cnt + cnt_sw
    base_ref[...] = new_base
    cnt_ref[...] = new_base

    cols = [i1.astype(F32), i2.astype(F32), r1, r2, w1, w2]
    route = jnp.zeros(logits.shape, F32)
    for c, val in enumerate(cols):
        route = jnp.where(lane == c, val, route)
    route_ref[0] = route


def _post(xall, o1, o2, o1_col, o2_col, wo, mod3, layer, g, wr, br, tri, n_lat_tiles, bsz):
    _, t, d = xall.shape
    half = wo.shape[0] // 2
    tok = lambda b, i: (b, i, 0)
    const = lambda b, i: (0, 0)
    wr_hi = wr.astype(BF16)
    wr_lo = (wr - wr_hi.astype(F32)).astype(BF16)
    return pl.pallas_call(
        _post_kernel,
        out_shape=[jax.ShapeDtypeStruct((bsz, t, d), F32), jax.ShapeDtypeStruct((bsz, t, LANE), F32),
                   jax.ShapeDtypeStruct((1, LANE), F32)],
        grid=(bsz, t // TM),
        in_specs=[
            pl.BlockSpec((1, TM, d), tok),
            pl.BlockSpec((1, TM, half), lambda b, i: (b, i, o1_col)),
            pl.BlockSpec((1, TM, half), lambda b, i: (b, i, o2_col)),
            pl.BlockSpec(wo.shape, const),
            pl.BlockSpec((1, 6, d), lambda b, i: (layer * 8 + jnp.where(i >= n_lat_tiles, bsz, b), 0, 0)),
            pl.BlockSpec(g.shape, const),
            pl.BlockSpec(wr.shape, const),
            pl.BlockSpec(wr.shape, const),
            pl.BlockSpec(br.shape, const),
            pl.BlockSpec(tri.shape, const),
        ],
        out_specs=[pl.BlockSpec((1, TM, d), tok), pl.BlockSpec((1, TM, LANE), tok),
                   pl.BlockSpec((1, LANE), const)],
        scratch_shapes=[pltpu.VMEM((1, LANE), F32)],
        compiler_params=_cparams(2),
    )(xall, o1, o2, wo, mod3, g, wr_hi, wr_lo, br, tri)


def _dispatch_kernel(dest_ref, x_ref, mod_ref, g_ref, xs_in_ref, xs_ref, hbuf, sem):
    del xs_in_ref
    mod = mod_ref[0]
    h = _modulate(x_ref[0], g_ref[...], mod[3:4], mod[4:5])
    bits = lax.bitcast_convert_type(h.astype(BF16).astype(F32), jnp.uint32)
    half = bits.shape[1] // 2
    hbuf[...] = (bits[:, :half] >> 16) | bits[:, half:]

    def row_copy(j, dst):
        return pltpu.make_async_copy(hbuf.at[pl.ds(j, 1)], xs_ref.at[pl.ds(dst, 1)], sem)

    def issue(j, c):
        for k in range(2):
            row_copy(j, dest_ref[0, 0, k * TM + j]).start()
        return c

    lax.fori_loop(0, TM, issue, 0, unroll=DMA_UNROLL)

    def drain(j, c):
        for k in range(2):
            row_copy(j, dest_ref[0, 0, k * TM + j]).wait()
        return c

    lax.fori_loop(0, TM, drain, 0, unroll=DMA_UNROLL)


def _dispatch(dest, xall, mod3, layer, g, total, n_lat_tiles, bsz):
    _, t, d = xall.shape
    nt = t // TM
    xs0 = jnp.zeros((total, d // 2), jnp.uint32)
    return pl.pallas_call(
        _dispatch_kernel,
        out_shape=jax.ShapeDtypeStruct((total, d // 2), jnp.uint32),
        grid=(bsz, nt),
        in_specs=[
            pl.BlockSpec((1, 1, 2 * TM), lambda b, i: (b * nt + i, 0, 0), memory_space=pltpu.SMEM),
            pl.BlockSpec((1, TM, d), lambda b, i: (b, i, 0)),
            pl.BlockSpec((1, 6, d), lambda b, i: (layer * 8 + jnp.where(i >= n_lat_tiles, bsz, b), 0, 0)),
            pl.BlockSpec(g.shape, lambda b, i: (0, 0)),
            pl.BlockSpec(memory_space=pl.ANY),
        ],
        out_specs=pl.BlockSpec(memory_space=pl.ANY),
        scratch_shapes=[pltpu.VMEM((TM, d // 2), jnp.uint32), pltpu.SemaphoreType.DMA(())],
        input_output_aliases={4: 0},
        compiler_params=_cparams(2),
    )(dest, xall, mod3, g, xs0)


def _expert_kernel(be_ref, nb_ref, xs_ref, wg_ref, wu_ref, wd_ref, ys_ref, wgb, wub, wdb):
    i = pl.program_id(0)
    changed = (i == 0) | (be_ref[i] != be_ref[jnp.maximum(i - 1, 0)])

    @pl.when(changed)
    def _():
        wgb[...] = wg_ref[0, 0].astype(BF16)
        wub[...] = wu_ref[0, 0].astype(BF16)
        wdb[...] = wd_ref[0, 0].astype(BF16)

    @pl.when(i < nb_ref[0])
    def _():
        hrows = MOE_BM // 2
        gu = []
        for hh in range(2):
            pk = xs_ref[hh * hrows:(hh + 1) * hrows, :]
            x = jnp.concatenate([lax.bitcast_convert_type(pk << 16, F32),
                                 lax.bitcast_convert_type(pk & jnp.uint32(0xFFFF0000), F32)],
                                axis=1).astype(BF16)
            gu.append((_dot(x, wgb[...]), _dot(x, wub[...])))
        for hh in range(2):
            a = _silu(gu[hh][0]) * gu[hh][1]
            ys_ref[hh * hrows:(hh + 1) * hrows, :] = _dot(a.astype(BF16), wdb[...])

    @pl.when(i >= nb_ref[0])
    def _():
        ys_ref[...] = jnp.zeros_like(ys_ref)


def _experts(blk_e, nb_used, xs, w_gate, w_up, w_down, layer):
    total = xs.shape[0]
    d, de = w_gate.shape[-2:]
    nblk = total // MOE_BM
    return pl.pallas_call(
        _expert_kernel,
        out_shape=jax.ShapeDtypeStruct((total, d), F32),
        grid_spec=pltpu.PrefetchScalarGridSpec(
            num_scalar_prefetch=2,
            grid=(nblk,),
            in_specs=[
                pl.BlockSpec((MOE_BM, d // 2), lambda i, be, nb: (i, 0)),
                pl.BlockSpec((1, 1, d, de), lambda i, be, nb: (layer, be[i], 0, 0)),
                pl.BlockSpec((1, 1, d, de), lambda i, be, nb: (layer, be[i], 0, 0)),
                pl.BlockSpec((1, 1, de, d), lambda i, be, nb: (layer, be[i], 0, 0)),
            ],
            out_specs=pl.BlockSpec((MOE_BM, d), lambda i, be, nb: (i, 0)),
            scratch_shapes=[pltpu.VMEM((d, de), BF16), pltpu.VMEM((d, de), BF16), pltpu.VMEM((de, d), BF16)],
        ),
        compiler_params=_cparams(1),
    )(blk_e, nb_used, xs, w_gate, w_up, w_down)


def _combine_kernel(dest_ref, x_ref, route_ref, mod_ref, gf_ref, ys_ref, o_ref, ybuf, sem, *, final):
    d = x_ref.shape[-1]

    def row_copy(j, k, src):
        return pltpu.make_async_copy(ys_ref.at[pl.ds(src, 1)], ybuf.at[pl.ds(j, 1), pl.ds(k * d, d)], sem)

    def issue(j, c):
        for k in range(2):
            row_copy(j, k, dest_ref[0, 0, k * TM + j]).start()
        return c

    lax.fori_loop(0, TM, issue, 0, unroll=DMA_UNROLL)

    def drain(j, c):
        for k in range(2):
            row_copy(j, k, dest_ref[0, 0, k * TM + j]).wait()
        return c

    lax.fori_loop(0, TM, drain, 0, unroll=DMA_UNROLL)

    route = route_ref[0]
    y = ybuf[:, 0:d] * route[:, 4:5] + ybuf[:, d:2 * d] * route[:, 5:6]
    xn = x_ref[0] + mod_ref[0][5:6] * y
    if final:
        xn = _rms(xn) * gf_ref[...]
    o_ref[0] = xn


def _combine(dest, xall, route, mod3, layer, gf, ys, t_out, final, n_lat_tiles, bsz):
    _, t, d = xall.shape
    nt = t // TM
    tok = lambda b, i: (b, i, 0)
    return pl.pallas_call(
        functools.partial(_combine_kernel, final=final),
        out_shape=jax.ShapeDtypeStruct((bsz, t_out, d), F32),
        grid=(bsz, t_out // TM),
        in_specs=[
            pl.BlockSpec((1, 1, 2 * TM), lambda b, i: (b * nt + i, 0, 0), memory_space=pltpu.SMEM),
            pl.BlockSpec((1, TM, d), tok),
            pl.BlockSpec((1, TM, LANE), tok),
            pl.BlockSpec((1, 6, d), lambda b, i: (layer * 8 + jnp.where(i >= n_lat_tiles, bsz, b), 0, 0)),
            pl.BlockSpec(gf.shape, lambda b, i: (0, 0)),
            pl.BlockSpec(memory_space=pl.ANY),
        ],
        out_specs=pl.BlockSpec((1, TM, d), tok),
        scratch_shapes=[pltpu.VMEM((TM, 2 * d), F32), pltpu.SemaphoreType.DMA(())],
        compiler_params=_cparams(2),
    )(dest, xall, route, mod3, gf, ys)


def _deint(n):
    return np.concatenate([np.arange(0, n, 2), np.arange(1, n, 2)])


def _rope_tables(s_len, ctx_len):
    tpos = jnp.arange(s_len, dtype=jnp.int32)
    r = (tpos // GRID_W).astype(F32)
    col = (tpos % GRID_W).astype(F32)

    def cos_sin(rot):
        n = rot // 4
        inv = ROPE_THETA ** (-jnp.arange(n, dtype=F32) / n)
        ang = jnp.concatenate([r[:, None] * inv, col[:, None] * inv], axis=-1)
        return jnp.cos(ang), jnp.sin(ang)

    def finish(cos_t, sinm_t, sinp_t, cos_ctx):
        zc = jnp.zeros((ctx_len, LANE), F32)
        return jnp.concatenate([
            jnp.concatenate([cos_t, sinm_t, sinp_t], axis=1),
            jnp.concatenate([cos_ctx, zc, zc], axis=1)], axis=0)

    ca, sa = cos_sin(ROPE_A)
    ha = ROPE_A // 2
    one = jnp.ones((s_len, NOPE_A), F32)
    z = lambda n: jnp.zeros((s_len, n), F32)
    tab_a = finish(
        jnp.concatenate([one, ca, ca, z(LANE - NOPE_A - ROPE_A)], axis=1),
        jnp.concatenate([z(NOPE_A), -sa, z(LANE - NOPE_A - ha)], axis=1),
        jnp.concatenate([z(NOPE_A + ha), sa, z(LANE - NOPE_A - ROPE_A)], axis=1),
        jnp.concatenate([jnp.ones((ctx_len, NOPE_A + ROPE_A), F32),
                         jnp.zeros((ctx_len, LANE - NOPE_A - ROPE_A), F32)], axis=1))
    cb, sb = cos_sin(HEAD_DIM)
    hb = HEAD_DIM // 2
    tab_b = finish(
        jnp.concatenate([cb, cb, cb, cb], axis=1),
        jnp.concatenate([-sb, z(hb), -sb, z(hb)], axis=1),
        jnp.concatenate([z(hb), sb, z(hb), sb], axis=1),
        jnp.ones((ctx_len, LANE), F32))
    return tab_a, tab_b


def _ab_weights(w_in, q_norm, w_q_up, kv_norm, w_kv_up, gq_norm, gk_norm):
    d = w_in.shape[0]
    o0 = Q_LORA
    o1 = o0 + KV_LORA
    o2 = o1 + ROPE_A
    o3 = o2 + H_B * HEAD_DIM
    o4 = o3 + KV_B * HEAD_DIM
    da = _deint(ROPE_A)
    db = _deint(HEAD_DIM)
    zeros = lambda *s: jnp.zeros(s, w_in.dtype)
    kr = jnp.concatenate([zeros(d, NOPE_A), w_in[:, o1:o2][:, da], zeros(d, LANE - NOPE_A - ROPE_A)], axis=1)
    gq = w_in[:, o2:o3].reshape(d, H_B, HEAD_DIM)[:, :, db]
    gq_blocks = []
    gqn_blocks = []
    gn = gq_norm[db]
    zn = jnp.zeros((HEAD_DIM,), gq_norm.dtype)
    for h in range(H_B):
        first = (h // G_B) == 0
        gq_blocks += [gq[:, h], zeros(d, HEAD_DIM)] if first else [zeros(d, HEAD_DIM), gq[:, h]]
        gqn_blocks += [gn, zn] if first else [zn, gn]
    gk = w_in[:, o3:o4].reshape(d, KV_B, HEAD_DIM)[:, :, db].reshape(d, KV_B * HEAD_DIM)
    win = jnp.concatenate([w_in[:, :o1], kr] + gq_blocks + [gk], axis=1).astype(BF16)
    assert win.shape[1] == _C_END
    pad_rows = VT_ROWS - LANE
    wgvt = jnp.concatenate([w_in[:, o4:].T, jnp.zeros((pad_rows, d), w_in.dtype)], axis=0).astype(BF16)
    wq = w_q_up.reshape(Q_LORA, H_A, NOPE_A + ROPE_A)
    wq = jnp.concatenate([wq[:, :, :NOPE_A], wq[:, :, NOPE_A:][:, :, da],
                          jnp.zeros((Q_LORA, H_A, LANE - NOPE_A - ROPE_A), w_q_up.dtype)], axis=2)
    wkv = w_kv_up.reshape(KV_LORA, H_A, NOPE_A + V_A)
    wk = jnp.concatenate([wkv[:, :, :NOPE_A], jnp.zeros((KV_LORA, H_A, LANE - NOPE_A), w_kv_up.dtype)], axis=2)
    wvt = wkv[:, :, NOPE_A:].reshape(KV_LORA, H_A // 2, 2 * V_A).transpose(1, 2, 0)
    wvt = jnp.concatenate([wvt, jnp.zeros((H_A // 2, pad_rows, KV_LORA), w_kv_up.dtype)], axis=1)
    return {
        "win": win,
        "qn": q_norm.reshape(1, -1),
        "wq": wq.reshape(Q_LORA, H_A * LANE).astype(BF16),
        "kvn": kv_norm.reshape(1, -1),
        "wk": wk.reshape(KV_LORA, H_A * LANE).astype(BF16),
        "wvt": wvt.reshape(H_A // 2 * VT_ROWS, KV_LORA).astype(BF16),
        "wgvt": wgvt,
        "gqn": jnp.concatenate(gqn_blocks).reshape(1, -1),
        "gkn": jnp.tile(gk_norm[db], KV_B).reshape(1, -1),
    }


def _moe_layout(route, cnt, bsz, nt, total):
    counts = cnt[0, :N_EXPERTS].astype(jnp.int32)
    padded = ((counts + MOE_BM - 1) // MOE_BM) * MOE_BM
    ends = jnp.cumsum(padded)
    pad_start = ends - padded
    e = route[..., 0:2].astype(jnp.int32)
    rank = route[..., 2:4].astype(jnp.int32)
    eid = jnp.arange(N_EXPERTS, dtype=jnp.int32)
    dest = jnp.sum(jnp.where(e[..., None] == eid, pad_start, 0), axis=-1) + rank
    dest = dest.reshape(bsz, nt, TM, 2).transpose(0, 1, 3, 2).reshape(bsz * nt, 1, 2 * TM)
    nblk = total // MOE_BM
    blk_start = jnp.arange(nblk, dtype=jnp.int32) * MOE_BM
    blk_e = jnp.sum((ends[None, :] <= blk_start[:, None]).astype(jnp.int32), axis=-1)
    blk_e = jnp.clip(blk_e, 0, N_EXPERTS - 1)
    nb_used = (ends[-1] // MOE_BM).astype(jnp.int32).reshape(1)
    return dest, blk_e, nb_used


def kernel(x, c, ctx, c_ctx, w_ada, b_ada, norm_mix, norm_ffn, norm_final, ab_w_in, ab_w_out, mla_q_norm,
           mla_w_q_up, mla_kv_norm, mla_w_kv_up, gqa_q_norm, gqa_k_norm, na_w_in, na_w_out, na_rpb, moe_w_rg,
           moe_b_rg, moe_w_re, moe_b_re, moe_w_gate, moe_w_up, moe_w_down):
    bsz, s_len, d = x.shape
    ctx_len = ctx.shape[1]
    depth = w_ada.shape[0]
    t = s_len + ctx_len
    assert s_len % TM == 0 and ctx_len == TM and s_len % GRID_W == 0 and bsz + 1 <= 8
    nt = t // TM
    n_lat_tiles = s_len // TM
    n_tok = bsz * t
    total = -(-(n_tok * 2 + N_EXPERTS * (MOE_BM - 1)) // MOE_BM) * MOE_BM

    cond = jnp.concatenate([c, c_ctx[None, :], jnp.zeros((8 - bsz - 1, d), c.dtype)], axis=0)
    mod = _ada_mod(cond.T, w_ada, b_ada, bsz + 1)
    mod3 = mod.reshape(depth * 8, 6, d)

    tabs = _rope_tables(s_len, ctx_len)
    tri = jnp.asarray(np.tril(np.ones((TM, TM), np.float32), -1), BF16)
    xall = jnp.concatenate([x, ctx], axis=1)

    for l in range(depth):
        last = l == depth - 1
        i = l // 2
        g_mix = norm_mix[l].reshape(1, d)
        g_ffn = norm_ffn[l].reshape(1, d)
        if l % 2 == 0:
            w = _ab_weights(ab_w_in[i], mla_q_norm[i], mla_w_q_up[i], mla_kv_norm[i], mla_w_kv_up[i],
                            gqa_q_norm[i], gqa_k_norm[i])
            qa, ka, va, qb, kb, vb = _ab_proj(xall, mod3, l, g_mix, w, tabs, n_lat_tiles, bsz)
            o1 = _flash(qa, ka, va, s_len, True)
            o2 = _flash(qb, kb, vb, s_len, False)
            o1_col = o2_col = 0
            wo = ab_w_out[i].astype(BF16)
        else:
            q, k, v = _na_proj(xall, mod3, l, g_mix, na_w_in[i].astype(BF16), n_lat_tiles, bsz)
            o1 = o2 = _na_attention(q, k, v, _na_bias_table(na_rpb[i]), s_len)
            o1_col, o2_col = 0, 1
            wo = na_w_out[i].astype(BF16)

        wr = jnp.concatenate([moe_w_re[l], moe_w_rg[l],
                              jnp.zeros((d, LANE - N_EXPERTS - N_GROUPS), F32)], axis=1).astype(F32)
        br = jnp.concatenate([moe_b_re[l], moe_b_rg[l],
                              jnp.zeros((LANE - N_EXPERTS - N_GROUPS,), F32)]).reshape(1, LANE).astype(F32)
        xall, route, cnt = _post(xall, o1, o2, o1_col, o2_col, wo, mod3, l, g_ffn, wr, br, tri,
                                 n_lat_tiles, bsz)
        dest, blk_e, nb_used = _moe_layout(route, cnt, bsz, nt, total)
        xs = _dispatch(dest, xall, mod3, l, g_ffn, total, n_lat_tiles, bsz)
        ys = _experts(blk_e, nb_used, xs, moe_w_gate, moe_w_up, moe_w_down, l)
        t_out = s_len if last else t
        xall = _combine(dest, xall, route, mod3, l, norm_final.reshape(1, d), ys, t_out, last,
                        n_lat_tiles, bsz)
    return xall
```

```python
import functools

import numpy as np
import jax
import jax.numpy as jnp
from jax import lax
from jax.experimental import pallas as pl
from jax.experimental.pallas import tpu as pltpu

F32 = jnp.float32
BF16 = jnp.bfloat16

GRID_W = 64
HEAD_DIM = 64
ROPE_THETA = 10000.0
RMS_EPS = 1e-6
H_A = 8
NOPE_A = 64
ROPE_A = 32
V_A = 64
Q_LORA = 384
KV_LORA = 256
H_B = 8
KV_B = 2
G_B = H_B // KV_B
H_C = 16
WIN_R = 8
WIN_C = 16
N_GROUPS = 4
EXPERTS_PER_GROUP = 8
N_EXPERTS = N_GROUPS * EXPERTS_PER_GROUP
D_EXPERT = 512

LANE = 128
TM = 256
MOE_BM = 512
NA_ROWS = TM // GRID_W
NEG = -1e30
LOG2E = 1.4426950408889634
FLASH_TK = 1280
FLASH_HEADS = 2
DMA_UNROLL = 8
VMEM_LIMIT = 48 * 1024 * 1024

_C_CQ = 0
_C_CKV = _C_CQ + Q_LORA
_C_KR = _C_CKV + KV_LORA
_C_GQ = _C_KR + LANE
_C_GK = _C_GQ + H_B * LANE
_C_END = _C_GK + KV_B * HEAD_DIM

VT_ROWS = LANE + 16


def _cparams(n_axes, vmem=VMEM_LIMIT):
    return pltpu.CompilerParams(dimension_semantics=("arbitrary",) * n_axes, vmem_limit_bytes=vmem)


def _rms(x):
    return x * lax.rsqrt(jnp.mean(x * x, axis=-1, keepdims=True) + RMS_EPS)


def _modulate(x, g, shift, scale):
    return (_rms(x) * g) * (1.0 + scale) + shift


def _silu(x):
    return x / (1.0 + jnp.exp(-x))


def _dot(a, b):
    return jnp.dot(a, b, preferred_element_type=F32)


def _dot_nt(a, b):
    return lax.dot_general(a, b, (((1,), (1,)), ((), ())), preferred_element_type=F32)


def _ada_kernel(c_ref, w_ref, b_ref, o_ref, *, n_vec):
    s = _silu(c_ref[...])
    w = w_ref[0]
    rows = []
    for r in range(8):
        if r < n_vec:
            rows.append(jnp.sum(w * s[:, r:r + 1], axis=0, keepdims=True) + b_ref[0])
        else:
            rows.append(jnp.zeros_like(b_ref[0]))
    o_ref[0] = jnp.concatenate(rows, axis=0)


def _ada_mod(cond_t, w_ada, b_ada, n_vec):
    depth, d, n6 = w_ada.shape
    tn = 512
    return pl.pallas_call(
        functools.partial(_ada_kernel, n_vec=n_vec),
        out_shape=jax.ShapeDtypeStruct((depth, 8, n6), F32),
        grid=(depth, n6 // tn),
        in_specs=[
            pl.BlockSpec((d, 8), lambda l, j: (0, 0)),
            pl.BlockSpec((1, d, tn), lambda l, j: (l, 0, j)),
            pl.BlockSpec((1, 1, tn), lambda l, j: (l, 0, j)),
        ],
        out_specs=pl.BlockSpec((1, 8, tn), lambda l, j: (l, 0, j)),
        compiler_params=_cparams(2),
    )(cond_t, w_ada, b_ada.reshape(depth, 1, n6))


def _rope(x, tab, sh):
    return (x * tab[:, 0:LANE]
            + pltpu.roll(x, LANE - sh, 1) * tab[:, LANE:2 * LANE]
            + pltpu.roll(x, sh, 1) * tab[:, 2 * LANE:3 * LANE])


def _with_ones_rows(vt):
    row = lax.broadcasted_iota(jnp.int32, vt.shape, 0)
    return jnp.where(row % VT_ROWS >= LANE, 1.0, vt)


def _ab_proj_kernel(x_ref, mod_ref, g_ref, win_ref, qn_ref, wq_ref, kvn_ref, wk_ref, wvt_ref, wgvt_ref,
                    gqn_ref, gkn_ref, ta_ref, tb_ref,
                    qa_ref, ka_ref, vat_ref, qb_ref, kb_ref, vbt_ref):
    mod = mod_ref[0]
    h = _modulate(x_ref[0], g_ref[...], mod[0:1], mod[1:2]).astype(BF16)
    p = _dot(h, win_ref[...])
    ta = ta_ref[...]
    tb = tb_ref[...]
    scale_a = (NOPE_A + ROPE_A) ** -0.5 * LOG2E
    scale_b = HEAD_DIM ** -0.5 * LOG2E

    cqn = (_rms(p[:, _C_CQ:_C_CKV]) * qn_ref[...]).astype(BF16)
    qa = _dot(cqn, wq_ref[...])
    ckvn = (_rms(p[:, _C_CKV:_C_KR]) * kvn_ref[...]).astype(BF16)
    ka = _dot(ckvn, wk_ref[...])
    vat_ref[0, 0] = _with_ones_rows(_dot_nt(wvt_ref[...], ckvn)).astype(BF16)
    kr = _rope(p[:, _C_KR:_C_GQ], ta, ROPE_A // 2)
    for hh in range(H_A):
        sl = slice(hh * LANE, (hh + 1) * LANE)
        qa_ref[0, :, sl] = (_rope(qa[:, sl], ta, ROPE_A // 2) * scale_a).astype(BF16)
        ka_ref[0, :, sl] = (ka[:, sl] + kr).astype(BF16)

    for hh in range(H_B):
        sl = slice(hh * LANE, (hh + 1) * LANE)
        xh = p[:, _C_GQ + hh * LANE:_C_GQ + (hh + 1) * LANE]
        r = lax.rsqrt(jnp.sum(xh * xh, axis=-1, keepdims=True) * (1.0 / HEAD_DIM) + RMS_EPS)
        qb_ref[0, :, sl] = (_rope(xh * r * gqn_ref[:, sl], tb, HEAD_DIM // 2) * scale_b).astype(BF16)
    gk = p[:, _C_GK:_C_END]
    lo = lax.broadcasted_iota(jnp.int32, gk.shape, 1) < HEAD_DIM
    sq = gk * gk
    s0 = jnp.sum(jnp.where(lo, sq, 0.0), axis=-1, keepdims=True)
    s1 = jnp.sum(jnp.where(lo, 0.0, sq), axis=-1, keepdims=True)
    r = jnp.where(lo, lax.rsqrt(s0 * (1.0 / HEAD_DIM) + RMS_EPS), lax.rsqrt(s1 * (1.0 / HEAD_DIM) + RMS_EPS))
    kb_ref[0] = _rope(gk * r * gkn_ref[...], tb, HEAD_DIM // 2).astype(BF16)
    vbt_ref[0, 0] = _with_ones_rows(_dot_nt(wgvt_ref[...], h)).astype(BF16)


def _ab_proj(xall, mod3, layer, g, w, tabs, n_lat_tiles, bsz):
    _, t, d = xall.shape
    nt = t // TM
    const = lambda b, i: (0, 0)
    tok = lambda b, i: (b, i, 0)

    def mod_map(b, i):
        return (layer * 8 + jnp.where(i >= n_lat_tiles, bsz, b), 0, 0)

    def full(a):
        return pl.BlockSpec(a.shape, const)

    def tok_out(n):
        return jax.ShapeDtypeStruct((bsz, t, n), BF16), pl.BlockSpec((1, TM, n), tok)

    def vt_out(rows):
        return (jax.ShapeDtypeStruct((bsz, nt, rows, TM), BF16),
                pl.BlockSpec((1, 1, rows, TM), lambda b, i: (b, i, 0, 0)))

    outs = [tok_out(H_A * LANE), tok_out(H_A * LANE), vt_out(H_A // 2 * VT_ROWS),
            tok_out(H_B * LANE), tok_out(KV_B * HEAD_DIM), vt_out(VT_ROWS)]
    return pl.pallas_call(
        _ab_proj_kernel,
        out_shape=[o[0] for o in outs],
        grid=(bsz, nt),
        in_specs=[
            pl.BlockSpec((1, TM, d), tok),
            pl.BlockSpec((1, 6, d), mod_map),
            full(g), full(w["win"]), full(w["qn"]), full(w["wq"]), full(w["kvn"]), full(w["wk"]),
            full(w["wvt"]), full(w["wgvt"]), full(w["gqn"]), full(w["gkn"]),
            pl.BlockSpec((TM, 3 * LANE), lambda b, i: (i, 0)),
            pl.BlockSpec((TM, 3 * LANE), lambda b, i: (i, 0)),
        ],
        out_specs=[o[1] for o in outs],
        compiler_params=_cparams(2),
    )(xall, mod3, g, w["win"], w["qn"], w["wq"], w["kvn"], w["wk"], w["wvt"], w["wgvt"], w["gqn"], w["gkn"],
      tabs[0], tabs[1])


def _flash_kernel(q_ref, k_ref, vt_ref, o_ref, st_ref, *, n_keys, tk, k_per_head):
    nh = FLASH_HEADS
    qs = [q_ref[0, :, a * LANE:(a + 1) * LANE] for a in range(nh)]
    tiles_per_chunk = tk // TM
    n_chunks = n_keys // tk

    def vt_rows(a):
        r0 = (a // 2) * VT_ROWS if k_per_head else 0
        return slice(r0, r0 + VT_ROWS)

    def scores(start, size, a):
        kc0 = a * LANE if k_per_head else 0
        return _dot_nt(k_ref[0, pl.ds(start, size), kc0:kc0 + LANE], qs[a])

    def update(st, vt, state):
        m, acc = state
        m_new = jnp.maximum(m, jnp.max(st, axis=0, keepdims=True))
        alpha = jnp.exp2(m - m_new)
        pexp = jnp.exp2(st - m_new).astype(BF16)
        return m_new, alpha * acc + _dot(vt, pexp)

    def qk_chunk(j, buf):
        for a in range(nh):
            st_ref[buf, a] = scores(pl.multiple_of(j * tk, tk), tk, a)

    def pv_chunk(j, buf, carry):
        vt = jnp.concatenate([vt_ref[0, j * tiles_per_chunk + u] for u in range(tiles_per_chunk)], axis=1)
        return tuple(update(st_ref[buf, a], vt[vt_rows(a)], carry[a]) for a in range(nh))

    init = (jnp.full((1, TM), NEG, F32), jnp.zeros((VT_ROWS, TM), F32))
    qk_chunk(0, 0)

    def pair(jj, carry):
        c0 = 2 * jj
        qk_chunk(c0 + 1, 1)
        carry = pv_chunk(c0, 0, carry)
        qk_chunk(jnp.minimum(c0 + 2, n_chunks - 1), 0)
        return pv_chunk(c0 + 1, 1, carry)

    full = lax.fori_loop(0, n_chunks // 2, pair, (init,) * nh, unroll=3)
    if n_chunks % 2:
        full = pv_chunk(n_chunks - 1, 0, full)
    hd = HEAD_DIM
    kv_lo = (pl.program_id(1) * nh) // G_B == 0
    outs = []
    for a, (_, acc) in enumerate(full):
        o = acc[0:LANE] / acc[LANE:LANE + 1]
        if k_per_head:
            outs.append(o[(a % 2) * hd:(a % 2 + 1) * hd])
        else:
            outs.append(jnp.where(kv_lo, o[0:hd], o[hd:2 * hd]))
    o_ref[0] = jnp.concatenate(outs, axis=0).T.astype(BF16)


def _flash(q, k, vt, s_len, k_per_head):
    bsz, t, qw = q.shape
    nh = FLASH_HEADS
    assert G_B % nh == 0 or nh % G_B == 0
    n_groups = qw // (nh * LANE)
    nt = t // TM
    n_lat = s_len // TM
    assert nt == n_lat + 1
    f = max(d for d in range(1, nt + 1) if nt % d == 0 and d * TM <= FLASH_TK)
    kw = nh * LANE if k_per_head else LANE
    vrows = nh // 2 * VT_ROWS if k_per_head else VT_ROWS
    kcol = (lambda g: g) if k_per_head else (lambda g: 0)

    def call(q_tile0, n_q_tiles, key_tile0, n_key_tiles, tk):
        n_keys = n_key_tiles * TM
        once = pl.Buffered(1)
        in_specs = [
            pl.BlockSpec((1, TM, nh * LANE), lambda b, g, i: (b, q_tile0 + i, g)),
            pl.BlockSpec((1, n_keys, kw), lambda b, g, i: (b, key_tile0 // n_key_tiles, kcol(g)),
                         pipeline_mode=once),
            pl.BlockSpec((1, n_key_tiles, vrows, TM), lambda b, g, i: (b, key_tile0 // n_key_tiles, kcol(g), 0),
                         pipeline_mode=once),
        ]
        return pl.pallas_call(
            functools.partial(_flash_kernel, n_keys=n_keys, tk=tk, k_per_head=k_per_head),
            out_shape=jax.ShapeDtypeStruct((bsz, n_q_tiles * TM, n_groups * nh * HEAD_DIM), BF16),
            grid=(bsz, n_groups, n_q_tiles),
            in_specs=in_specs,
            out_specs=pl.BlockSpec((1, TM, nh * HEAD_DIM), lambda b, g, i: (b, i, g)),
            scratch_shapes=[pltpu.VMEM((2, nh, tk, TM), F32)],
            compiler_params=_cparams(3),
        )(q, k, vt)

    return jnp.concatenate([call(0, n_lat, 0, nt, f * TM), call(n_lat, 1, n_lat, 1, TM)], axis=1)


def _na_proj_kernel(x_ref, mod_ref, g_ref, w_ref, q_ref, k_ref, v_ref):
    mod = mod_ref[0]
    h = _modulate(x_ref[0], g_ref[...], mod[0:1], mod[1:2]).astype(BF16)
    p = _dot(h, w_ref[...])
    cw = H_C * HEAD_DIM
    q_ref[0] = (p[:, 0:cw] * (HEAD_DIM ** -0.5 * LOG2E)).astype(BF16)
    k_ref[0] = p[:, cw:2 * cw].astype(BF16)
    v_ref[0] = p[:, 2 * cw:3 * cw].astype(BF16)


def _na_proj(xall, mod3, layer, g, w, n_lat_tiles, bsz):
    _, t, d = xall.shape
    cw = H_C * HEAD_DIM
    tok = lambda b, i: (b, i, 0)
    return pl.pallas_call(
        _na_proj_kernel,
        out_shape=[jax.ShapeDtypeStruct((bsz, t, cw), BF16)] * 3,
        grid=(bsz, t // TM),
        in_specs=[
            pl.BlockSpec((1, TM, d), tok),
            pl.BlockSpec((1, 6, d), lambda b, i: (layer * 8 + jnp.where(i >= n_lat_tiles, bsz, b), 0, 0)),
            pl.BlockSpec(g.shape, lambda b, i: (0, 0)),
            pl.BlockSpec(w.shape, lambda b, i: (0, 0)),
        ],
        out_specs=[pl.BlockSpec((1, TM, cw), tok)] * 3,
        compiler_params=_cparams(2),
    )(xall, mod3, g, w)


def _na_kernel(q_ref, k_ref, v_ref, bias_ref, o_ref, *, n_lat_tiles, s_len, ctx_len, rows_n):
    i = pl.program_id(2)
    nwin = WIN_R * GRID_W
    gw = GRID_W
    kc = k_ref[0, s_len:s_len + ctx_len, :]
    vc = v_ref[0, s_len:s_len + ctx_len, :]
    q2 = q_ref[0]
    lo = lax.broadcasted_iota(jnp.int32, (TM, LANE), 1) < HEAD_DIM
    zero = jnp.zeros_like(q2)
    qs = jnp.concatenate([jnp.where(lo, q2, zero), jnp.where(lo, zero, q2)], axis=0)
    sc_all = _dot_nt(qs, kc)

    def unstack(o):
        n = o.shape[0] // 2
        lo_n = lax.broadcasted_iota(jnp.int32, (n, LANE), 1) < HEAD_DIM
        return jnp.where(lo_n, o[0:n], o[n:2 * n])

    @pl.when(i < n_lat_tiles)
    def _():
        o_n, p_c, l_inv = [], [], []
        for rr in range(NA_ROWS):
            r = i * NA_ROWS + rr
            rs = jnp.clip(r - WIN_R // 2, 0, rows_n - WIN_R)
            var = r - rs
            start = pl.multiple_of(rs * gw, gw)
            kwin = k_ref[0, pl.ds(start, nwin), :]
            vwin = v_ref[0, pl.ds(start, nwin), :]
            rows = [slice(a * TM + rr * gw, a * TM + (rr + 1) * gw) for a in range(2)]
            q_r = jnp.concatenate([qs[rows[0]], qs[rows[1]]], axis=0)
            sn = _dot_nt(q_r, kwin) + jnp.concatenate([bias_ref[var, 0], bias_ref[var, 1]], axis=0)
            sc = jnp.concatenate([sc_all[rows[0]], sc_all[rows[1]]], axis=0)
            m = jnp.maximum(jnp.max(sn, axis=-1, keepdims=True), jnp.max(sc, axis=-1, keepdims=True))
            pn = jnp.exp2(sn - m)
            pc = jnp.exp2(sc - m)
            l_inv.append(1.0 / (jnp.sum(pn, axis=-1, keepdims=True) + jnp.sum(pc, axis=-1, keepdims=True)))
            o_n.append(_dot(pn.astype(BF16), vwin))
            p_c.append(pc.astype(BF16))
        o_c = _dot(jnp.concatenate(p_c, axis=0), vc)
        for rr in range(NA_ROWS):
            o = (o_n[rr] + o_c[rr * 2 * gw:(rr + 1) * 2 * gw]) * l_inv[rr]
            o_ref[0, rr * gw:(rr + 1) * gw, :] = unstack(o).astype(BF16)

    @pl.when(i >= n_lat_tiles)
    def _():
        m = jnp.max(sc_all, axis=-1, keepdims=True)
        pc = jnp.exp2(sc_all - m)
        l = jnp.sum(pc, axis=-1, keepdims=True)
        o_ref[0] = unstack(_dot(pc.astype(BF16), vc) / l).astype(BF16)


def _na_attention(q, k, v, bias, s_len):
    bsz, t, cw = q.shape
    n_pairs = cw // LANE
    rows_n = s_len // GRID_W
    assert rows_n >= WIN_R and GRID_W >= WIN_C
    tile = lambda b, p, i: (b, i, p)
    res = lambda b, p, i: (b, 0, p)
    return pl.pallas_call(
        functools.partial(_na_kernel, n_lat_tiles=s_len // TM, s_len=s_len, ctx_len=t - s_len, rows_n=rows_n),
        out_shape=jax.ShapeDtypeStruct((bsz, t, cw), BF16),
        grid=(bsz, n_pairs, t // TM),
        in_specs=[
            pl.BlockSpec((1, TM, LANE), tile),
            pl.BlockSpec((1, t, LANE), res),
            pl.BlockSpec((1, t, LANE), res),
            pl.BlockSpec((WIN_R, 2, GRID_W, WIN_R * GRID_W), lambda b, p, i: (0, p, 0, 0)),
        ],
        out_specs=pl.BlockSpec((1, TM, LANE), tile),
        compiler_params=_cparams(3),
    )(q, k, v, bias)


def _na_bias_table(rpb):
    rpb = rpb.astype(F32)
    rows = jnp.stack([rpb[:, WIN_R - 1 - v:2 * WIN_R - 1 - v, :] for v in range(WIN_R)], axis=0)
    rp = jnp.pad(rows, ((0, 0), (0, 0), (0, 0), (GRID_W, GRID_W)))
    off = GRID_W + WIN_C - 1
    vals = jnp.stack([rp[..., off - c:off - c + GRID_W] for c in range(GRID_W)], axis=2)
    col = np.arange(GRID_W)[:, None, None]
    kc = np.arange(GRID_W)[None, None, :]
    cs = np.clip(col - WIN_C // 2, 0, GRID_W - WIN_C)
    valid = np.broadcast_to((kc >= cs) & (kc < cs + WIN_C), (GRID_W, WIN_R, GRID_W))
    bias = jnp.where(valid[None, None], vals * LOG2E, NEG)
    return bias.reshape(WIN_R, rpb.shape[0], GRID_W, WIN_R * GRID_W)


def _post_kernel(x_ref, o1_ref, o2_ref, wo_ref, mod_ref, g_ref, wrh_ref, wrl_ref, br_ref, tri_ref,
                 xo_ref, route_ref, cnt_ref, base_ref):
    @pl.when((pl.program_id(0) == 0) & (pl.program_id(1) == 0))
    def _():
        base_ref[...] = jnp.zeros_like(base_ref)

    mod = mod_ref[0]
    half = wo_ref.shape[0] // 2
    o = _dot(o1_ref[0], wo_ref[0:half, :]) + _dot(o2_ref[0], wo_ref[half:2 * half, :])
    xn = x_ref[0] + mod[2:3] * o
    xo_ref[0] = xn

    h2 = _modulate(xn, g_ref[...], mod[3:4], mod[4:5])
    h_hi = h2.astype(BF16)
    h_lo = (h2 - h_hi.astype(F32)).astype(BF16)
    logits = (_dot(h_hi, wrh_ref[...]) + (_dot(h_lo, wrh_ref[...]) + _dot(h_hi, wrl_ref[...]))) + br_ref[...]
    half_l = LANE // 2

    def choose(lgt):
        lane = lax.broadcasted_iota(jnp.int32, lgt.shape, 1)
        isg = (lane >= N_EXPERTS) & (lane < N_EXPERTS + N_GROUPS)
        lg = jnp.where(isg, lgt, NEG)
        eg = jnp.where(isg, jnp.exp(lg - jnp.max(lg, axis=-1, keepdims=True)), 0.0)
        pg = eg / jnp.sum(eg, axis=-1, keepdims=True)
        p_gsel = jnp.max(pg, axis=-1, keepdims=True)
        gsel = jnp.min(jnp.where(isg & (pg == p_gsel), lane - N_EXPERTS, N_GROUPS), axis=-1, keepdims=True)
        sel = (lane < N_EXPERTS) & ((lane // EXPERTS_PER_GROUP) == gsel)
        le = jnp.where(sel, lgt, NEG)
        ee = jnp.where(sel, jnp.exp(le - jnp.max(le, axis=-1, keepdims=True)), 0.0)
        pe = ee / jnp.sum(ee, axis=-1, keepdims=True)
        pm = jnp.where(sel, pe, -1.0)
        p1 = jnp.max(pm, axis=-1, keepdims=True)
        i1 = jnp.min(jnp.where(pm == p1, lane, LANE), axis=-1, keepdims=True)
        pm2 = jnp.where(lane == i1, -2.0, pm)
        p2 = jnp.max(pm2, axis=-1, keepdims=True)
        i2 = jnp.min(jnp.where(pm2 == p2, lane, LANE), axis=-1, keepdims=True)
        den = p1 + p2
        return i1, i2, p_gsel * p1 / den, p_gsel * p2 / den

    i1, i2, w1, w2 = choose(logits)

    lane = lax.broadcasted_iota(jnp.int32, logits.shape, 1)
    oh1 = lane == i1
    oh2 = lane == i2 + half_l
    oh = jnp.where(oh1 | oh2, 1.0, 0.0)
    oh_sw = jnp.where((lane == i2) | (lane == i1 + half_l), 1.0, 0.0)
    prefix = _dot(tri_ref[...], oh.astype(BF16))
    cnt = jnp.sum(oh, axis=0, keepdims=True)
    cnt_sw = jnp.sum(oh_sw, axis=0, keepdims=True)
    base = base_ref[...]
    lane1 = lax.broadcasted_iota(jnp.int32, base.shape, 1)
    tot = prefix + base + jnp.where(lane1 >= half_l, cnt_sw, 0.0)
    r1 = jnp.sum(jnp.where(oh1, tot, 0.0), axis=-1, keepdims=True)
    r2 = jnp.sum(jnp.where(oh2, tot, 0.0), axis=-1, keepdims=True)
    new_base = base + cnt + cnt_sw
    base_ref[...] = new_base
    cnt_ref[...] = new_base

    cols = [i1.astype(F32), i2.astype(F32), r1, r2, w1, w2]
    route = jnp.zeros(logits.shape, F32)
    for c, val in enumerate(cols):
        route = jnp.where(lane == c, val, route)
    route_ref[0] = route


def _post(xall, o1, o2, o1_col, o2_col, wo, mod3, layer, g, wr, br, tri, n_lat_tiles, bsz):
    _, t, d = xall.shape
    half = wo.shape[0] // 2
    tok = lambda b, i: (b, i, 0)
    const = lambda b, i: (0, 0)
    wr_hi = wr.astype(BF16)
    wr_lo = (wr - wr_hi.astype(F32)).astype(BF16)
    return pl.pallas_call(
        _post_kernel,
        out_shape=[jax.ShapeDtypeStruct((bsz, t, d), F32), jax.ShapeDtypeStruct((bsz, t, LANE), F32),
                   jax.ShapeDtypeStruct((1, LANE), F32)],
        grid=(bsz, t // TM),
        in_specs=[
            pl.BlockSpec((1, TM, d), tok),
            pl.BlockSpec((1, TM, half), lambda b, i: (b, i, o1_col)),
            pl.BlockSpec((1, TM, half), lambda b, i: (b, i, o2_col)),
            pl.BlockSpec(wo.shape, const),
            pl.BlockSpec((1, 6, d), lambda b, i: (layer * 8 + jnp.where(i >= n_lat_tiles, bsz, b), 0, 0)),
            pl.BlockSpec(g.shape, const),
            pl.BlockSpec(wr.shape, const),
            pl.BlockSpec(wr.shape, const),
            pl.BlockSpec(br.shape, const),
            pl.BlockSpec(tri.shape, const),
        ],
        out_specs=[pl.BlockSpec((1, TM, d), tok), pl.BlockSpec((1, TM, LANE), tok),
                   pl.BlockSpec((1, LANE), const)],
        scratch_shapes=[pltpu.VMEM((1, LANE), F32)],
        compiler_params=_cparams(2),
    )(xall, o1, o2, wo, mod3, g, wr_hi, wr_lo, br, tri)


def _dispatch_kernel(dest_ref, x_ref, mod_ref, g_ref, xs_in_ref, xs_ref, hbuf, sem):
    del xs_in_ref
    mod = mod_ref[0]
    h = _modulate(x_ref[0], g_ref[...], mod[3:4], mod[4:5])
    bits = lax.bitcast_convert_type(h.astype(BF16).astype(F32), jnp.uint32)
    half = bits.shape[1] // 2
    hbuf[...] = (bits[:, :half] >> 16) | bits[:, half:]

    def row_copy(j, dst):
        return pltpu.make_async_copy(hbuf.at[pl.ds(j, 1)], xs_ref.at[pl.ds(dst, 1)], sem)

    def issue(j, c):
        for k in range(2):
            row_copy(j, dest_ref[0, 0, k * TM + j]).start(priority=k)
        return c

    lax.fori_loop(0, TM, issue, 0, unroll=DMA_UNROLL)

    def drain(j, c):
        for k in range(2):
            row_copy(j, dest_ref[0, 0, k * TM + j]).wait()
        return c

    lax.fori_loop(0, TM, drain, 0, unroll=DMA_UNROLL)


def _dispatch(dest, xall, mod3, layer, g, total, n_lat_tiles, bsz):
    _, t, d = xall.shape
    nt = t // TM
    xs0 = jnp.zeros((total, d // 2), jnp.uint32)
    return pl.pallas_call(
        _dispatch_kernel,
        out_shape=jax.ShapeDtypeStruct((total, d // 2), jnp.uint32),
        grid=(bsz, nt),
        in_specs=[
            pl.BlockSpec((1, 1, 2 * TM), lambda b, i: (b * nt + i, 0, 0), memory_space=pltpu.SMEM),
            pl.BlockSpec((1, TM, d), lambda b, i: (b, i, 0)),
            pl.BlockSpec((1, 6, d), lambda b, i: (layer * 8 + jnp.where(i >= n_lat_tiles, bsz, b), 0, 0)),
            pl.BlockSpec(g.shape, lambda b, i: (0, 0)),
            pl.BlockSpec(memory_space=pl.ANY),
        ],
        out_specs=pl.BlockSpec(memory_space=pl.ANY),
        scratch_shapes=[pltpu.VMEM((TM, d // 2), jnp.uint32), pltpu.SemaphoreType.DMA(())],
        input_output_aliases={4: 0},
        compiler_params=_cparams(2),
    )(dest, xall, mod3, g, xs0)


def _expert_kernel(be_ref, nb_ref, xs_ref, wg_ref, wu_ref, wd_ref, ys_ref, wgb, wub, wdb):
    i = pl.program_id(0)
    changed = (i == 0) | (be_ref[i] != be_ref[jnp.maximum(i - 1, 0)])

    @pl.when(changed)
    def _():
        wgb[...] = wg_ref[0, 0].astype(BF16)
        wub[...] = wu_ref[0, 0].astype(BF16)
        wdb[...] = wd_ref[0, 0].astype(BF16)

    @pl.when(i < nb_ref[0])
    def _():
        hrows = MOE_BM // 2
        gu = []
        for hh in range(2):
            pk = xs_ref[hh * hrows:(hh + 1) * hrows, :]
            x = jnp.concatenate([lax.bitcast_convert_type(pk << 16, F32),
                                 lax.bitcast_convert_type(pk & jnp.uint32(0xFFFF0000), F32)],
                                axis=1).astype(BF16)
            gu.append((_dot(x, wgb[...]), _dot(x, wub[...])))
        for hh in range(2):
            a = _silu(gu[hh][0]) * gu[hh][1]
            ys_ref[hh * hrows:(hh + 1) * hrows, :] = _dot(a.astype(BF16), wdb[...])

    @pl.when(i >= nb_ref[0])
    def _():
        ys_ref[...] = jnp.zeros_like(ys_ref)


def _experts(blk_e, nb_used, xs, w_gate, w_up, w_down, layer):
    total = xs.shape[0]
    d, de = w_gate.shape[-2:]
    nblk = total // MOE_BM
    return pl.pallas_call(
        _expert_kernel,
        out_shape=jax.ShapeDtypeStruct((total, d), F32),
        grid_spec=pltpu.PrefetchScalarGridSpec(
            num_scalar_prefetch=2,
            grid=(nblk,),
            in_specs=[
                pl.BlockSpec((MOE_BM, d // 2), lambda i, be, nb: (i, 0)),
                pl.BlockSpec((1, 1, d, de), lambda i, be, nb: (layer, be[i], 0, 0)),
                pl.BlockSpec((1, 1, d, de), lambda i, be, nb: (layer, be[i], 0, 0)),
                pl.BlockSpec((1, 1, de, d), lambda i, be, nb: (layer, be[i], 0, 0)),
            ],
            out_specs=pl.BlockSpec((MOE_BM, d), lambda i, be, nb: (i, 0)),
            scratch_shapes=[pltpu.VMEM((d, de), BF16), pltpu.VMEM((d, de), BF16), pltpu.VMEM((de, d), BF16)],
        ),
        compiler_params=_cparams(1),
    )(blk_e, nb_used, xs, w_gate, w_up, w_down)


def _combine_kernel(dest_ref, x_ref, route_ref, mod_ref, gf_ref, ys_ref, o_ref, ybuf, sem, *, final):
    d = x_ref.shape[-1]

    def row_copy(j, k, src):
        return pltpu.make_async_copy(ys_ref.at[pl.ds(src, 1)], ybuf.at[pl.ds(j, 1), pl.ds(k * d, d)], sem)

    def issue(j, c):
        for k in range(2):
            row_copy(j, k, dest_ref[0, 0, k * TM + j]).start(priority=k)
        return c

    lax.fori_loop(0, TM, issue, 0, unroll=DMA_UNROLL)

    def drain(j, c):
        for k in range(2):
            row_copy(j, k, dest_ref[0, 0, k * TM + j]).wait()
        return c

    lax.fori_loop(0, TM, drain, 0, unroll=DMA_UNROLL)

    route = route_ref[0]
    y = ybuf[:, 0:d] * route[:, 4:5] + ybuf[:, d:2 * d] * route[:, 5:6]
    xn = x_ref[0] + mod_ref[0][5:6] * y
    if final:
        xn = _rms(xn) * gf_ref[...]
    o_ref[0] = xn


def _combine(dest, xall, route, mod3, layer, gf, ys, t_out, final, n_lat_tiles, bsz):
    _, t, d = xall.shape
    nt = t // TM
    tok = lambda b, i: (b, i, 0)
    return pl.pallas_call(
        functools.partial(_combine_kernel, final=final),
        out_shape=jax.ShapeDtypeStruct((bsz, t_out, d), F32),
        grid=(bsz, t_out // TM),
        in_specs=[
            pl.BlockSpec((1, 1, 2 * TM), lambda b, i: (b * nt + i, 0, 0), memory_space=pltpu.SMEM),
            pl.BlockSpec((1, TM, d), tok),
            pl.BlockSpec((1, TM, LANE), tok),
            pl.BlockSpec((1, 6, d), lambda b, i: (layer * 8 + jnp.where(i >= n_lat_tiles, bsz, b), 0, 0)),
            pl.BlockSpec(gf.shape, lambda b, i: (0, 0)),
            pl.BlockSpec(memory_space=pl.ANY),
        ],
        out_specs=pl.BlockSpec((1, TM, d), tok),
        scratch_shapes=[pltpu.VMEM((TM, 2 * d), F32), pltpu.SemaphoreType.DMA(())],
        compiler_params=_cparams(2),
    )(dest, xall, route, mod3, gf, ys)


def _deint(n):
    return np.concatenate([np.arange(0, n, 2), np.arange(1, n, 2)])


def _rope_tables(s_len, ctx_len):
    tpos = jnp.arange(s_len, dtype=jnp.int32)
    r = (tpos // GRID_W).astype(F32)
    col = (tpos % GRID_W).astype(F32)

    def cos_sin(rot):
        n = rot // 4
        inv = ROPE_THETA ** (-jnp.arange(n, dtype=F32) / n)
        ang = jnp.concatenate([r[:, None] * inv, col[:, None] * inv], axis=-1)
        return jnp.cos(ang), jnp.sin(ang)

    def finish(cos_t, sinm_t, sinp_t, cos_ctx):
        zc = jnp.zeros((ctx_len, LANE), F32)
        return jnp.concatenate([
            jnp.concatenate([cos_t, sinm_t, sinp_t], axis=1),
            jnp.concatenate([cos_ctx, zc, zc], axis=1)], axis=0)

    ca, sa = cos_sin(ROPE_A)
    ha = ROPE_A // 2
    one = jnp.ones((s_len, NOPE_A), F32)
    z = lambda n: jnp.zeros((s_len, n), F32)
    tab_a = finish(
        jnp.concatenate([one, ca, ca, z(LANE - NOPE_A - ROPE_A)], axis=1),
        jnp.concatenate([z(NOPE_A), -sa, z(LANE - NOPE_A - ha)], axis=1),
        jnp.concatenate([z(NOPE_A + ha), sa, z(LANE - NOPE_A - ROPE_A)], axis=1),
        jnp.concatenate([jnp.ones((ctx_len, NOPE_A + ROPE_A), F32),
                         jnp.zeros((ctx_len, LANE - NOPE_A - ROPE_A), F32)], axis=1))
    cb, sb = cos_sin(HEAD_DIM)
    hb = HEAD_DIM // 2
    tab_b = finish(
        jnp.concatenate([cb, cb, cb, cb], axis=1),
        jnp.concatenate([-sb, z(hb), -sb, z(hb)], axis=1),
        jnp.concatenate([z(hb), sb, z(hb), sb], axis=1),
        jnp.ones((ctx_len, LANE), F32))
    return tab_a, tab_b


def _ab_weights(w_in, q_norm, w_q_up, kv_norm, w_kv_up, gq_norm, gk_norm):
    d = w_in.shape[0]
    o0 = Q_LORA
    o1 = o0 + KV_LORA
    o2 = o1 + ROPE_A
    o3 = o2 + H_B * HEAD_DIM
    o4 = o3 + KV_B * HEAD_DIM
    da = _deint(ROPE_A)
    db = _deint(HEAD_DIM)
    zeros = lambda *s: jnp.zeros(s, w_in.dtype)
    kr = jnp.concatenate([zeros(d, NOPE_A), w_in[:, o1:o2][:, da], zeros(d, LANE - NOPE_A - ROPE_A)], axis=1)
    gq = w_in[:, o2:o3].reshape(d, H_B, HEAD_DIM)[:, :, db]
    gq_blocks = []
    gqn_blocks = []
    gn = gq_norm[db]
    zn = jnp.zeros((HEAD_DIM,), gq_norm.dtype)
    for h in range(H_B):
        first = (h // G_B) == 0
        gq_blocks += [gq[:, h], zeros(d, HEAD_DIM)] if first else [zeros(d, HEAD_DIM), gq[:, h]]
        gqn_blocks += [gn, zn] if first else [zn, gn]
    gk = w_in[:, o3:o4].reshape(d, KV_B, HEAD_DIM)[:, :, db].reshape(d, KV_B * HEAD_DIM)
    win = jnp.concatenate([w_in[:, :o1], kr] + gq_blocks + [gk], axis=1).astype(BF16)
    assert win.shape[1] == _C_END
    pad_rows = VT_ROWS - LANE
    wgvt = jnp.concatenate([w_in[:, o4:].T, jnp.zeros((pad_rows, d), w_in.dtype)], axis=0).astype(BF16)
    wq = w_q_up.reshape(Q_LORA, H_A, NOPE_A + ROPE_A)
    wq = jnp.concatenate([wq[:, :, :NOPE_A], wq[:, :, NOPE_A:][:, :, da],
                          jnp.zeros((Q_LORA, H_A, LANE - NOPE_A - ROPE_A), w_q_up.dtype)], axis=2)
    wkv = w_kv_up.reshape(KV_LORA, H_A, NOPE_A + V_A)
    wk = jnp.concatenate([wkv[:, :, :NOPE_A], jnp.zeros((KV_LORA, H_A, LANE - NOPE_A), w_kv_up.dtype)], axis=2)
    wvt = wkv[:, :, NOPE_A:].reshape(KV_LORA, H_A // 2, 2 * V_A).transpose(1, 2, 0)
    wvt = jnp.concatenate([wvt, jnp.zeros((H_A // 2, pad_rows, KV_LORA), w_kv_up.dtype)], axis=1)
    return {
        "win": win,
        "qn": q_norm.reshape(1, -1),
        "wq": wq.reshape(Q_LORA, H_A * LANE).astype(BF16),
        "kvn": kv_norm.reshape(1, -1),
        "wk": wk.reshape(KV_LORA, H_A * LANE).astype(BF16),
        "wvt": wvt.reshape(H_A // 2 * VT_ROWS, KV_LORA).astype(BF16),
        "wgvt": wgvt,
        "gqn": jnp.concatenate(gqn_blocks).reshape(1, -1),
        "gkn": jnp.tile(gk_norm[db], KV_B).reshape(1, -1),
    }


def _moe_layout(route, cnt, bsz, nt, total):
    counts = cnt[0, :N_EXPERTS].astype(jnp.int32)
    padded = ((counts + MOE_BM - 1) // MOE_BM) * MOE_BM
    ends = jnp.cumsum(padded)
    pad_start = ends - padded
    e = route[..., 0:2].astype(jnp.int32)
    rank = route[..., 2:4].astype(jnp.int32)
    eid = jnp.arange(N_EXPERTS, dtype=jnp.int32)
    dest = jnp.sum(jnp.where(e[..., None] == eid, pad_start, 0), axis=-1) + rank
    dest = dest.reshape(bsz, nt, TM, 2).transpose(0, 1, 3, 2).reshape(bsz * nt, 1, 2 * TM)
    nblk = total // MOE_BM
    blk_start = jnp.arange(nblk, dtype=jnp.int32) * MOE_BM
    blk_e = jnp.sum((ends[None, :] <= blk_start[:, None]).astype(jnp.int32), axis=-1)
    blk_e = jnp.clip(blk_e, 0, N_EXPERTS - 1)
    nb_used = (ends[-1] // MOE_BM).astype(jnp.int32).reshape(1)
    return dest, blk_e, nb_used


def kernel(x, c, ctx, c_ctx, w_ada, b_ada, norm_mix, norm_ffn, norm_final, ab_w_in, ab_w_out, mla_q_norm,
           mla_w_q_up, mla_kv_norm, mla_w_kv_up, gqa_q_norm, gqa_k_norm, na_w_in, na_w_out, na_rpb, moe_w_rg,
           moe_b_rg, moe_w_re, moe_b_re, moe_w_gate, moe_w_up, moe_w_down):
    bsz, s_len, d = x.shape
    ctx_len = ctx.shape[1]
    depth = w_ada.shape[0]
    t = s_len + ctx_len
    assert s_len % TM == 0 and ctx_len == TM and s_len % GRID_W == 0 and bsz + 1 <= 8
    nt = t // TM
    n_lat_tiles = s_len // TM
    n_tok = bsz * t
    total = -(-(n_tok * 2 + N_EXPERTS * (MOE_BM - 1)) // MOE_BM) * MOE_BM

    cond = jnp.concatenate([c, c_ctx[None, :], jnp.zeros((8 - bsz - 1, d), c.dtype)], axis=0)
    mod = _ada_mod(cond.T, w_ada, b_ada, bsz + 1)
    mod3 = mod.reshape(depth * 8, 6, d)

    tabs = _rope_tables(s_len, ctx_len)
    tri = jnp.asarray(np.tril(np.ones((TM, TM), np.float32), -1), BF16)
    xall = jnp.concatenate([x, ctx], axis=1)

    for l in range(depth):
        last = l == depth - 1
        i = l // 2
        g_mix = norm_mix[l].reshape(1, d)
        g_ffn = norm_ffn[l].reshape(1, d)
        if l % 2 == 0:
            w = _ab_weights(ab_w_in[i], mla_q_norm[i], mla_w_q_up[i], mla_kv_norm[i], mla_w_kv_up[i],
                            gqa_q_norm[i], gqa_k_norm[i])
            qa, ka, va, qb, kb, vb = _ab_proj(xall, mod3, l, g_mix, w, tabs, n_lat_tiles, bsz)
            o1 = _flash(qa, ka, va, s_len, True)
            o2 = _flash(qb, kb, vb, s_len, False)
            o1_col = o2_col = 0
            wo = ab_w_out[i].astype(BF16)
        else:
            q, k, v = _na_proj(xall, mod3, l, g_mix, na_w_in[i].astype(BF16), n_lat_tiles, bsz)
            o1 = o2 = _na_attention(q, k, v, _na_bias_table(na_rpb[i]), s_len)
            o1_col, o2_col = 0, 1
            wo = na_w_out[i].astype(BF16)

        wr = jnp.concatenate([moe_w_re[l], moe_w_rg[l],
                              jnp.zeros((d, LANE - N_EXPERTS - N_GROUPS), F32)], axis=1).astype(F32)
        br = jnp.concatenate([moe_b_re[l], moe_b_rg[l],
                              jnp.zeros((LANE - N_EXPERTS - N_GROUPS,), F32)]).reshape(1, LANE).astype(F32)
        xall, route, cnt = _post(xall, o1, o2, o1_col, o2_col, wo, mod3, l, g_ffn, wr, br, tri,
                                 n_lat_tiles, bsz)
        dest, blk_e, nb_used = _moe_layout(route, cnt, bsz, nt, total)
        xs = _dispatch(dest, xall, mod3, l, g_ffn, total, n_lat_tiles, bsz)
        ys = _experts(blk_e, nb_used, xs, moe_w_gate, moe_w_up, moe_w_down, l)
        t_out = s_len if last else t
        xall = _combine(dest, xall, route, mod3, l, norm_final.reshape(1, d), ys, t_out, last,
                        n_lat_tiles, bsz)
    return xall
```
